```python
import math
import jax, jax.numpy as jnp
from jax import lax
import numpy as np

D_MODEL = 2048
BATCH = 8
SEQ = 2048
DEPTH = 2

HEAD_DIM = 64
N_Q_HEADS = D_MODEL // (2 * HEAD_DIM)
N_KV_HEADS = N_Q_HEADS // 4
WINDOW = 128
ROPE_THETA = 10000.0
SSM_HEADS = D_MODEL // (2 * HEAD_DIM)
SSM_HEAD_DIM = 64
D_INNER = SSM_HEADS * SSM_HEAD_DIM
D_STATE = 128
N_GROUPS = 2
CONV_WIDTH = 4
CHUNK = 128
RWKV_HEADS = D_MODEL // HEAD_DIM
DECAY_LORA = 96
AAA_LORA = 96
GATE_LORA = 256
N_SHIFT_MIX = 6
D_FF = 4 * D_MODEL
NORM_EPS = 1e-6
GN_EPS = 64e-5

Q_W = N_Q_HEADS * HEAD_DIM
KV_W = N_KV_HEADS * HEAD_DIM
BC_W = N_GROUPS * D_STATE
CONV_DIM = D_INNER + 2 * BC_W
IN_W = Q_W + 2 * KV_W + D_INNER + CONV_DIM + SSM_HEADS
MIX_W = Q_W + D_INNER
N_EVEN = (DEPTH + 1) // 2
N_ODD = DEPTH // 2

kernel_name = "hybrid_swa_ssd_rwkv7_trunk"


def rms_norm(x, g, eps=NORM_EPS):
    xf = x.astype(jnp.float32)
    y = xf * lax.rsqrt(jnp.mean(xf * xf, axis=-1, keepdims=True) + eps)
    return (y * g.astype(jnp.float32)).astype(x.dtype)


def apply_rope(x, cos, sin):
    x1, x2 = jnp.split(x, 2, axis=-1)
    c = cos[None, :, None, :].astype(x.dtype)
    s = sin[None, :, None, :].astype(x.dtype)
    return jnp.concatenate([x1 * c - x2 * s, x2 * c + x1 * s], axis=-1)


def sliding_window_attention(q, k, v, sinks):
    b_, L = q.shape[0], q.shape[1]
    nb = L // WINDOW
    rep = N_Q_HEADS // N_KV_HEADS
    qb = q.reshape(b_, nb, WINDOW, N_KV_HEADS, rep, HEAD_DIM)
    kb = k.reshape(b_, nb, WINDOW, N_KV_HEADS, HEAD_DIM)
    vb = v.reshape(b_, nb, WINDOW, N_KV_HEADS, HEAD_DIM)

    def with_prev(t):
        prev = jnp.concatenate([jnp.zeros_like(t[:, :1]), t[:, :-1]], axis=1)
        return jnp.concatenate([prev, t], axis=2)

    kw, vw = with_prev(kb), with_prev(vb)
    s = jnp.einsum('bnqhrd,bnkhd->bnhrqk', qb, kw).astype(jnp.float32) * (HEAD_DIM ** -0.5)
    qi = jnp.arange(WINDOW)[:, None] + WINDOW
    kj = jnp.arange(2 * WINDOW)[None, :]
    band = (kj <= qi) & (qi - kj < WINDOW)
    first = band & (kj >= WINDOW)
    mask = jnp.where((jnp.arange(nb) == 0)[:, None, None], first[None], band[None])
    s = jnp.where(mask[None, :, None, None], s, -jnp.inf)
    sink = sinks.astype(jnp.float32).reshape(N_KV_HEADS, rep)[None, None, :, :, None, None]
    m = jnp.maximum(jnp.max(s, axis=-1, keepdims=True), sink)
    p = jnp.exp(s - m)
    p = p / (jnp.sum(p, axis=-1, keepdims=True) + jnp.exp(sink - m))
    o = jnp.einsum('bnhrqk,bnkhd->bnqhrd', p.astype(v.dtype), vw)
    return o.reshape(b_, L, Q_W)


def causal_depthwise_conv(x, w, b):
    y = lax.conv_general_dilated(
        x, w[:, None, :].astype(x.dtype), window_strides=(1,),
        padding=[(CONV_WIDTH - 1, 0)], dimension_numbers=('NWC', 'WIO', 'NWC'),
        feature_group_count=x.shape[-1])
    return y + b.astype(x.dtype)


def ssd_chunked(xs, dt, A, Bm, Cm):
    b_, L = xs.shape[0], xs.shape[1]
    nc = L // CHUNK
    hg = SSM_HEADS // N_GROUPS
    x = (xs.astype(jnp.float32) * dt[..., None]).reshape(b_, nc, CHUNK, N_GROUPS, hg, SSM_HEAD_DIM)
    Bc = Bm.astype(jnp.float32).reshape(b_, nc, CHUNK, N_GROUPS, D_STATE)
    Cc = Cm.astype(jnp.float32).reshape(b_, nc, CHUNK, N_GROUPS, D_STATE)
    cs = jnp.cumsum((dt * A).reshape(b_, nc, CHUNK, N_GROUPS, hg), axis=2)
    causal = jnp.tril(jnp.ones((CHUNK, CHUNK), dtype=bool))[None, None, :, :, None, None]
    seg = cs[:, :, :, None] - cs[:, :, None]
    Lm = jnp.exp(jnp.where(causal, seg, -jnp.inf))
    CB = jnp.einsum('bcqgn,bckgn->bcqkg', Cc, Bc)
    y_diag = jnp.einsum('bcqkg,bcqkgh,bckghp->bcqghp', CB, Lm, x)
    decay_out = jnp.exp(cs[:, :, -1:] - cs)
    states = jnp.einsum('bckgn,bckgh,bckghp->bcghpn', Bc, decay_out, x)
    chunk_decay = jnp.exp(cs[:, :, -1])

    def step(h, inp):
        s_c, d_c = inp
        return h * d_c[..., None, None] + s_c, h

    h0 = jnp.zeros((b_, N_GROUPS, hg, SSM_HEAD_DIM, D_STATE), jnp.float32)
    _, prev = lax.scan(step, h0, (jnp.moveaxis(states, 1, 0), jnp.moveaxis(chunk_decay, 1, 0)))
    prev = jnp.moveaxis(prev, 0, 1)
    y_off = jnp.einsum('bcqgn,bcghpn,bcqgh->bcqghp', Cc, prev, jnp.exp(cs))
    return (y_diag + y_off).reshape(b_, L, SSM_HEADS, SSM_HEAD_DIM)


def attn_ssd_mixer(h, cos, sin, w_in, conv_w, conv_b, dt_bias, a_log, d_skip, ssd_norm,
                   q_norm, k_norm, sinks, w_out):
    b_, L, _ = h.shape
    proj = h @ w_in.astype(h.dtype)
    splits = [Q_W, Q_W + KV_W, Q_W + 2 * KV_W, Q_W + 2 * KV_W + D_INNER,
              Q_W + 2 * KV_W + D_INNER + CONV_DIM]
    q, k, v, z, xbc, dt_raw = jnp.split(proj, splits, axis=-1)
    q = apply_rope(rms_norm(q.reshape(b_, L, N_Q_HEADS, HEAD_DIM), q_norm), cos, sin)
    k = apply_rope(rms_norm(k.reshape(b_, L, N_KV_HEADS, HEAD_DIM), k_norm), cos, sin)
    v = v.reshape(b_, L, N_KV_HEADS, HEAD_DIM)
    y_attn = sliding_window_attention(q, k, v, sinks)
    xbc = jax.nn.silu(causal_depthwise_conv(xbc, conv_w, conv_b))
    xs, Bm, Cm = jnp.split(xbc, [D_INNER, D_INNER + BC_W], axis=-1)
    xs = xs.reshape(b_, L, SSM_HEADS, SSM_HEAD_DIM)
    Bm = Bm.reshape(b_, L, N_GROUPS, D_STATE)
    Cm = Cm.reshape(b_, L, N_GROUPS, D_STATE)
    dt = jax.nn.softplus(dt_raw.astype(jnp.float32) + dt_bias.astype(jnp.float32))
    A = -jnp.exp(a_log.astype(jnp.float32))
    y = ssd_chunked(xs, dt, A, Bm, Cm) + d_skip.astype(jnp.float32)[:, None] * xs.astype(jnp.float32)
    y = y.reshape(b_, L, D_INNER) * jax.nn.silu(z.astype(jnp.float32))
    yg = y.reshape(b_, L, N_GROUPS, D_INNER // N_GROUPS)
    yg = yg * lax.rsqrt(jnp.mean(yg * yg, axis=-1, keepdims=True) + NORM_EPS)
    y_ssd = (yg.reshape(b_, L, D_INNER) * ssd_norm.astype(jnp.float32)).astype(h.dtype)
    return jnp.concatenate([y_attn.astype(h.dtype), y_ssd], axis=-1) @ w_out.astype(h.dtype)


def rwkv7_scan(r, w, k, v, a, b):
    def step(S, inp):
        r_t, w_t, k_t, v_t, a_t, b_t = inp
        sa = jnp.einsum('bhvk,bhk->bhv', S, a_t)
        S = S * w_t[:, :, None, :] + sa[..., None] * b_t[:, :, None, :] + v_t[..., None] * k_t[:, :, None, :]
        return S, jnp.einsum('bhvk,bhk->bhv', S, r_t)

    seq = tuple(jnp.moveaxis(t.astype(jnp.float32), 1, 0) for t in (r, w, k, v, a, b))
    b_ = r.shape[0]
    S0 = jnp.zeros((b_, RWKV_HEADS, HEAD_DIM, HEAD_DIM), jnp.float32)
    _, y = lax.scan(step, S0, seq)
    return jnp.moveaxis(y, 0, 1)


def rwkv7_time_mix(h, mix, w0, w1, w2, a0, a1, a2, g1, g2, k_k, k_a, r_k,
                   w_r, w_k, w_v, w_o, ln_w, ln_b):
    b_, L, D = h.shape
    dt_ = h.dtype
    xx = jnp.pad(h, ((0, 0), (1, 0), (0, 0)))[:, :-1] - h
    mix = mix.astype(dt_)
    xr, xw, xk, xv, xa, xg = (h + xx * mix[i] for i in range(N_SHIFT_MIX))
    r = xr @ w_r.astype(dt_)
    k = xk @ w_k.astype(dt_)
    v = xv @ w_v.astype(dt_)
    logw = -jax.nn.softplus(-(w0.astype(jnp.float32)
                              + (jnp.tanh(xw @ w1.astype(dt_)) @ w2.astype(dt_)).astype(jnp.float32))) - 0.5
    decay = jnp.exp(-jnp.exp(logw))
    a = jax.nn.sigmoid((a0.astype(dt_) + (xa @ a1.astype(dt_)) @ a2.astype(dt_)).astype(jnp.float32))
    g = jax.nn.sigmoid(xg @ g1.astype(dt_)) @ g2.astype(dt_)
    kf = k.astype(jnp.float32)
    kk = (kf * k_k.astype(jnp.float32)).reshape(b_, L, RWKV_HEADS, HEAD_DIM)
    kk = kk * lax.rsqrt(jnp.maximum(jnp.sum(kk * kk, axis=-1, keepdims=True), 1e-24))
    kf = kf * (1.0 + (a - 1.0) * k_a.astype(jnp.float32))
    hs = lambda t: t.reshape(b_, L, RWKV_HEADS, HEAD_DIM)
    rh, kh, vh, ah = hs(r.astype(jnp.float32)), hs(kf), hs(v.astype(jnp.float32)), hs(a)
    y = rwkv7_scan(rh, hs(decay), kh, vh, -kk, kk * ah)
    mu = jnp.mean(y, axis=-1, keepdims=True)
    var = jnp.mean(jnp.square(y - mu), axis=-1, keepdims=True)
    y = ((y - mu) * lax.rsqrt(var + GN_EPS)).reshape(b_, L, D)
    y = hs(y * ln_w.astype(jnp.float32) + ln_b.astype(jnp.float32))
    bonus = jnp.sum(rh * kh * r_k.astype(jnp.float32), axis=-1, keepdims=True) * vh
    y = (y + bonus).reshape(b_, L, D).astype(dt_)
    return (y * g) @ w_o.astype(dt_)


def squared_relu_mlp(h, w_up, w_down):
    u = h @ w_up.astype(h.dtype)
    return jnp.square(jax.nn.relu(u)) @ w_down.astype(h.dtype)


def setup_inputs(seed: int = 0) -> dict:
    key = jax.random.key(seed)
    ks = iter(jax.random.split(key, 48))

    def nrm(shape, scale):
        return scale * jax.random.normal(next(ks), shape, jnp.float32)

    def uni(shape, lo, hi):
        return jax.random.uniform(next(ks), shape, jnp.float32, lo, hi)

    D = D_MODEL
    x = nrm((BATCH, SEQ, D), 1.0)
    norm_mix = 1.0 + nrm((DEPTH, D), 0.02)
    norm_ffn = 1.0 + nrm((DEPTH, D), 0.02)
    w_up = nrm((DEPTH, D, D_FF), D ** -0.5)
    w_down = nrm((DEPTH, D_FF, D), D_FF ** -0.5)
    ev_w_in = nrm((N_EVEN, D, IN_W), D ** -0.5)
    ev_conv_w = nrm((N_EVEN, CONV_WIDTH, CONV_DIM), CONV_WIDTH ** -0.5)
    ev_conv_b = nrm((N_EVEN, CONV_DIM), 0.02)
    dt0 = jnp.exp(uni((N_EVEN, SSM_HEADS), math.log(1e-3), math.log(1e-1)))
    ev_dt_bias = dt0 + jnp.log(-jnp.expm1(-dt0))
    ev_a_log = jnp.log(uni((N_EVEN, SSM_HEADS), 1.0, 16.0))
    ev_d_skip = 1.0 + nrm((N_EVEN, SSM_HEADS), 0.1)
    ev_ssd_norm = 1.0 + nrm((N_EVEN, D_INNER), 0.02)
    ev_q_norm = 1.0 + nrm((N_EVEN, HEAD_DIM), 0.02)
    ev_k_norm = 1.0 + nrm((N_EVEN, HEAD_DIM), 0.02)
    ev_sinks = nrm((N_EVEN, N_Q_HEADS), 0.5)
    ev_w_out = nrm((N_EVEN, MIX_W, D), MIX_W ** -0.5)
    od_mix = uni((N_ODD, N_SHIFT_MIX, D), 0.0, 1.0)
    od_w0 = uni((N_ODD, D), -6.0, -1.0)
    od_w1 = nrm((N_ODD, D, DECAY_LORA), D ** -0.5)
    od_w2 = nrm((N_ODD, DECAY_LORA, D), 0.1 * DECAY_LORA ** -0.5)
    od_a0 = nrm((N_ODD, D), 0.5)
    od_a1 = nrm((N_ODD, D, AAA_LORA), D ** -0.5)
    od_a2 = nrm((N_ODD, AAA_LORA, D), AAA_LORA ** -0.5)
    od_g1 = nrm((N_ODD, D, GATE_LORA), D ** -0.5)
    od_g2 = nrm((N_ODD, GATE_LORA, D), GATE_LORA ** -0.5)
    od_k_k = 0.85 + nrm((N_ODD, D), 0.05)
    od_k_a = 1.0 + nrm((N_ODD, D), 0.05)
    od_r_k = nrm((N_ODD, RWKV_HEADS, HEAD_DIM), 0.1)
    od_w_r = nrm((N_ODD, D, D), D ** -0.5)
    od_w_k = nrm((N_ODD, D, D), D ** -0.5)
    od_w_v = nrm((N_ODD, D, D), D ** -0.5)
    od_w_o = nrm((N_ODD, D, D), D ** -0.5)
    od_ln_w = 1.0 + nrm((N_ODD, D), 0.02)
    od_ln_b = nrm((N_ODD, D), 0.02)
    return {
        "x": x, "norm_mix": norm_mix, "norm_ffn": norm_ffn, "w_up": w_up, "w_down": w_down,
        "ev_w_in": ev_w_in, "ev_conv_w": ev_conv_w, "ev_conv_b": ev_conv_b,
        "ev_dt_bias": ev_dt_bias, "ev_a_log": ev_a_log, "ev_d_skip": ev_d_skip,
        "ev_ssd_norm": ev_ssd_norm, "ev_q_norm": ev_q_norm, "ev_k_norm": ev_k_norm,
        "ev_sinks": ev_sinks, "ev_w_out": ev_w_out,
        "od_mix": od_mix, "od_w0": od_w0, "od_w1": od_w1, "od_w2": od_w2,
        "od_a0": od_a0, "od_a1": od_a1, "od_a2": od_a2, "od_g1": od_g1, "od_g2": od_g2,
        "od_k_k": od_k_k, "od_k_a": od_k_a, "od_r_k": od_r_k,
        "od_w_r": od_w_r, "od_w_k": od_w_k, "od_w_v": od_w_v, "od_w_o": od_w_o,
        "od_ln_w": od_ln_w, "od_ln_b": od_ln_b,
    }


def reference(x, norm_mix, norm_ffn, w_up, w_down,
              ev_w_in, ev_conv_w, ev_conv_b, ev_dt_bias, ev_a_log, ev_d_skip,
              ev_ssd_norm, ev_q_norm, ev_k_norm, ev_sinks, ev_w_out,
              od_mix, od_w0, od_w1, od_w2, od_a0, od_a1, od_a2, od_g1, od_g2,
              od_k_k, od_k_a, od_r_k, od_w_r, od_w_k, od_w_v, od_w_o, od_ln_w, od_ln_b):
    L = x.shape[1]
    inv_freq = ROPE_THETA ** (-jnp.arange(0, HEAD_DIM, 2, dtype=jnp.float32) / HEAD_DIM)
    ang = jnp.arange(L, dtype=jnp.float32)[:, None] * inv_freq[None, :]
    cos, sin = jnp.cos(ang), jnp.sin(ang)
    for i in range(DEPTH):
        j = i // 2
        h = rms_norm(x, norm_mix[i])
        if i % 2 == 0:
            mix = attn_ssd_mixer(h, cos, sin, ev_w_in[j], ev_conv_w[j], ev_conv_b[j],
                                 ev_dt_bias[j], ev_a_log[j], ev_d_skip[j], ev_ssd_norm[j],
                                 ev_q_norm[j], ev_k_norm[j], ev_sinks[j], ev_w_out[j])
        else:
            mix = rwkv7_time_mix(h, od_mix[j], od_w0[j], od_w1[j], od_w2[j], od_a0[j],
                                 od_a1[j], od_a2[j], od_g1[j], od_g2[j], od_k_k[j], od_k_a[j],
                                 od_r_k[j], od_w_r[j], od_w_k[j], od_w_v[j], od_w_o[j],
                                 od_ln_w[j], od_ln_b[j])
        x = x + mix.astype(x.dtype)
        h = rms_norm(x, norm_ffn[i])
        x = x + squared_relu_mlp(h, w_up[i], w_down[i]).astype(x.dtype)
    return x
```

```python
import functools

import jax
import jax.numpy as jnp
from jax import lax
from jax.experimental import pallas as pl
from jax.experimental.pallas import tpu as pltpu

F32 = jnp.float32
BF16 = jnp.bfloat16
HIGHEST = lax.Precision.HIGHEST

D_MODEL = 2048
HEAD_DIM = 64
N_Q_HEADS = 16
N_KV_HEADS = 4
Q_REP = N_Q_HEADS // N_KV_HEADS
WINDOW = 128
ROPE_THETA = 10000.0
SSM_HEADS = 16
SSM_HEAD_DIM = 64
D_INNER = 1024
D_STATE = 128
N_GROUPS = 2
CONV_WIDTH = 4
CHUNK = 128
RWKV_HEADS = 32
D_FF = 4 * D_MODEL
NORM_EPS = 1e-6
GN_EPS = 64e-5
Q_W = 1024
KV_W = 256
BC_W = 256

LANES = 128
SUBLANES = 8
VMEM_LIMIT_BYTES = 56 * 1024 * 1024

PROJ_W = 3 * 1024 + 4 * 256 + LANES
RWKV_CHUNK = 64


def _cparams(*sem):
    return pltpu.CompilerParams(dimension_semantics=sem, vmem_limit_bytes=VMEM_LIMIT_BYTES)


def _rms(x, g):
    inv = lax.rsqrt(jnp.mean(x * x, axis=-1, keepdims=True) + NORM_EPS)
    return x * inv * g


def _silu(x):
    return x * (1.0 / (1.0 + jnp.exp(-x)))


def _sigmoid(x):
    return 1.0 / (1.0 + jnp.exp(-x))


def _softplus(x):
    return jnp.maximum(x, 0.0) + jnp.log1p(jnp.exp(-jnp.abs(x)))


def _dot(a, b):
    return jnp.dot(a, b, preferred_element_type=F32)


def _dot_nt(a, b):
    return lax.dot_general(a, b, (((1,), (1,)), ((), ())), preferred_element_type=F32)


def _dot_tn(a, b):
    return lax.dot_general(a, b, (((0,), (0,)), ((), ())), preferred_element_type=F32)


def _norm_matmul_kernel(x_ref, g_ref, w_ref, o_ref, h_ref):
    @pl.when(pl.program_id(1) == 0)
    def _():
        h_ref[...] = _rms(x_ref[...], g_ref[...]).astype(BF16)

    o_ref[...] = _dot(h_ref[...], w_ref[...])


def norm_matmul(x, g, w, *, tm, tn):
    t, d = x.shape
    n = w.shape[1]
    return pl.pallas_call(
        _norm_matmul_kernel,
        out_shape=jax.ShapeDtypeStruct((t, n), F32),
        grid=(t // tm, n // tn),
        in_specs=[pl.BlockSpec((tm, d), lambda i, j: (i, 0)),
                  pl.BlockSpec((1, d), lambda i, j: (0, 0)),
                  pl.BlockSpec((d, tn), lambda i, j: (0, j))],
        out_specs=pl.BlockSpec((tm, tn), lambda i, j: (i, j)),
        scratch_shapes=[pltpu.VMEM((tm, d), BF16)],
        compiler_params=_cparams("parallel", "arbitrary"),
        name="norm_matmul",
    )(x, g, w)


def _matmul_residual_kernel(*refs, n_in):
    res_ref = refs[0]
    a_refs = refs[1:1 + n_in]
    w_refs = refs[1 + n_in:1 + 2 * n_in]
    o_ref = refs[1 + 2 * n_in]
    acc = res_ref[...]
    for a_ref, w_ref in zip(a_refs, w_refs):
        acc = acc + _dot(a_ref[...], w_ref[...])
    o_ref[...] = acc


def matmul_residual(res, a_list, w_list, *, tm, tn):
    t, n = res.shape
    n_in = len(a_list)
    in_specs = [pl.BlockSpec((tm, tn), lambda i, j: (i, j))]
    in_specs += [pl.BlockSpec((tm, a.shape[1]), lambda i, j: (i, 0)) for a in a_list]
    in_specs += [pl.BlockSpec((w.shape[0], tn), lambda i, j: (0, j)) for w in w_list]
    return pl.pallas_call(
        functools.partial(_matmul_residual_kernel, n_in=n_in),
        out_shape=jax.ShapeDtypeStruct((t, n), F32),
        grid=(t // tm, n // tn),
        in_specs=in_specs,
        out_specs=pl.BlockSpec((tm, tn), lambda i, j: (i, j)),
        compiler_params=_cparams("parallel", "arbitrary"),
        name="matmul_residual",
    )(res, *a_list, *w_list)


def _mlp_kernel(x_ref, g_ref, wu_ref, wd_ref, o_ref, h_ref):
    @pl.when(pl.program_id(1) == 0)
    def _():
        x = x_ref[...]
        h_ref[...] = _rms(x, g_ref[...]).astype(BF16)
        o_ref[...] = x

    u = _dot(h_ref[...], wu_ref[...])
    a = jnp.square(jnp.maximum(u, 0.0)).astype(BF16)
    o_ref[...] += _dot(a, wd_ref[...])


def mlp_block(x, g, w_up, w_down, *, tm, tf):
    t, d = x.shape
    f = w_up.shape[1]
    return pl.pallas_call(
        _mlp_kernel,
        out_shape=jax.ShapeDtypeStruct((t, d), F32),
        grid=(t // tm, f // tf),
        in_specs=[pl.BlockSpec((tm, d), lambda i, j: (i, 0)),
                  pl.BlockSpec((1, d), lambda i, j: (0, 0)),
                  pl.BlockSpec((d, tf), lambda i, j: (0, j)),
                  pl.BlockSpec((tf, d), lambda i, j: (j, 0))],
        out_specs=pl.BlockSpec((tm, d), lambda i, j: (i, 0)),
        scratch_shapes=[pltpu.VMEM((tm, d), BF16)],
        compiler_params=_cparams("parallel", "arbitrary"),
        name="mlp_block",
    )(x, g, w_up, w_down)


def _rope(x, cos, sin):
    half = HEAD_DIM // 2
    rot = jnp.concatenate([x[:, half:], x[:, :half]], axis=1)
    return x * cos + rot * sin


def _attn_kernel(sinks_ref, q_ref, kc_ref, kp_ref, vc_ref, vp_ref, cc_ref, sc_ref, cp_ref, sp_ref,
                 qn_ref, kn_ref, o_ref):
    n = pl.program_id(1)
    w = WINDOW
    cos_c, sin_c, cos_p, sin_p = cc_ref[...], sc_ref[...], cp_ref[...], sp_ref[...]
    qn, kn = qn_ref[...], kn_ref[...]

    rows = lax.broadcasted_iota(jnp.int32, (Q_REP * w, 2 * w), 0)
    qi = rows % w
    kj = lax.broadcasted_iota(jnp.int32, (Q_REP * w, 2 * w), 1)
    lo = jnp.where(n == 0, w - 1, 0)
    mask = (kj > qi) & (kj <= qi + w) & (kj > lo)

    for g in range(N_KV_HEADS):
        ks = slice(g * HEAD_DIM, (g + 1) * HEAD_DIM)
        k_cur = _rope(_rms(kc_ref[:, ks], kn), cos_c, sin_c)
        k_prev = _rope(_rms(kp_ref[:, ks], kn), cos_p, sin_p)
        k_all = jnp.concatenate([k_prev, k_cur], axis=0).astype(BF16)
        v_all = jnp.concatenate([vp_ref[:, ks], vc_ref[:, ks]], axis=0).astype(BF16)
        qs, sink_cols = [], []
        for r in range(Q_REP):
            h = g * Q_REP + r
            qh = _rope(_rms(q_ref[:, h * HEAD_DIM:(h + 1) * HEAD_DIM], qn), cos_c, sin_c)
            qs.append(qh * (HEAD_DIM ** -0.5))
            sink_cols.append(jnp.full((w, 1), sinks_ref[h], F32))
        q_all = jnp.concatenate(qs, axis=0).astype(BF16)
        sink = jnp.concatenate(sink_cols, axis=0)
        s = _dot_nt(q_all, k_all)
        s = jnp.where(mask, s, -jnp.inf)
        m = jnp.maximum(jnp.max(s, axis=-1, keepdims=True), sink)
        p = jnp.exp(s - m)
        denom = jnp.sum(p, axis=-1, keepdims=True) + jnp.exp(sink - m)
        o = _dot(p.astype(BF16), v_all) / denom
        for r in range(Q_REP):
            h = g * Q_REP + r
            o_ref[:, h * HEAD_DIM:(h + 1) * HEAD_DIM] = o[r * w:(r + 1) * w].astype(o_ref.dtype)


def swa_attention(proj, cos_t, sin_t, q_norm, k_norm, sinks, *, batch, seq):
    nb = seq // WINDOW
    kcol = 3 * 1024 // KV_W
    vcol = kcol + 1

    def cur(b, n):
        return b * nb + n

    def prev(b, n):
        return b * nb + jnp.maximum(n - 1, 0)

    return pl.pallas_call(
        _attn_kernel,
        out_shape=jax.ShapeDtypeStruct((batch * seq, Q_W), BF16),
        grid=(batch, nb),
        in_specs=[pl.BlockSpec(memory_space=pltpu.SMEM),
                  pl.BlockSpec((WINDOW, Q_W), lambda b, n: (cur(b, n), 0)),
                  pl.BlockSpec((WINDOW, KV_W), lambda b, n: (cur(b, n), kcol)),
                  pl.BlockSpec((WINDOW, KV_W), lambda b, n: (prev(b, n), kcol)),
                  pl.BlockSpec((WINDOW, KV_W), lambda b, n: (cur(b, n), vcol)),
                  pl.BlockSpec((WINDOW, KV_W), lambda b, n: (prev(b, n), vcol)),
                  pl.BlockSpec((WINDOW, HEAD_DIM), lambda b, n: (n, 0)),
                  pl.BlockSpec((WINDOW, HEAD_DIM), lambda b, n: (n, 0)),
                  pl.BlockSpec((WINDOW, HEAD_DIM), lambda b, n: (jnp.maximum(n - 1, 0), 0)),
                  pl.BlockSpec((WINDOW, HEAD_DIM), lambda b, n: (jnp.maximum(n - 1, 0), 0)),
                  pl.BlockSpec((1, HEAD_DIM), lambda b, n: (0, 0)),
                  pl.BlockSpec((1, HEAD_DIM), lambda b, n: (0, 0))],
        out_specs=pl.BlockSpec((WINDOW, Q_W), lambda b, n: (cur(b, n), 0)),
        compiler_params=_cparams("parallel", "parallel"),
        name="swa_attention",
    )(sinks, proj, proj, proj, proj, proj, cos_t, sin_t, cos_t, sin_t, q_norm, k_norm)


def _ssd_kernel(z_ref, x_ref, b_ref, c_ref, dt_ref, cwx_ref, cwb_ref, cwc_ref, cbx_ref, cbb_ref, cbc_ref,
                dtb_ref, alog_ref, dsk_ref, nrm_ref, exp_ref, o_ref,
                xpx_ref, xpb_ref, xpc_ref, state_ref, y_ref):
    q = CHUNK
    hist = SUBLANES

    @pl.when(pl.program_id(1) == 0)
    def _():
        xpx_ref[0:hist, :] = jnp.zeros((hist, D_INNER), F32)
        xpb_ref[0:hist, :] = jnp.zeros((hist, BC_W), F32)
        xpc_ref[0:hist, :] = jnp.zeros((hist, BC_W), F32)
        state_ref[...] = jnp.zeros_like(state_ref)

    def conv(raw_ref, pad_ref, cw_ref, cb_ref):
        pad_ref[hist:hist + q, :] = raw_ref[...]
        acc = cb_ref[...]
        for j in range(CONV_WIDTH):
            start = hist - (CONV_WIDTH - 1) + j
            acc = acc + pad_ref[start:start + q, :] * cw_ref[j:j + 1, :]
        pad_ref[0:hist, :] = pad_ref[q:q + hist, :]
        return _silu(acc)

    xs = conv(x_ref, xpx_ref, cwx_ref, cbx_ref)
    bm = conv(b_ref, xpb_ref, cwb_ref, cbb_ref)
    cm = conv(c_ref, xpc_ref, cwc_ref, cbc_ref)

    dt = _softplus(dt_ref[...] + dtb_ref[...])
    dta = dt * (-jnp.exp(alog_ref[...]))
    ri = lax.broadcasted_iota(jnp.int32, (q, q), 0)
    ci = lax.broadcasted_iota(jnp.int32, (q, q), 1)
    causal = ri >= ci
    tril = jnp.where(causal, 1.0, 0.0).astype(F32)
    cs = jnp.dot(tril, dta, precision=HIGHEST, preferred_element_type=F32)
    cs_t = cs.T
    cs_last = cs[q - 1:q, :]
    expand = exp_ref[...]
    stacked = jnp.concatenate([dt, jnp.exp(cs), jnp.exp(cs_last - cs)], axis=0)
    wide = jnp.dot(stacked, expand, precision=HIGHEST, preferred_element_type=F32)
    dt_w, ecs_w, dout_w = wide[0:q], wide[q:2 * q], wide[2 * q:3 * q]
    cdec_w = ecs_w[q - 1:q, :]

    xdt = xs * dt_w
    xdt_b = xdt.astype(BF16)
    xdec_b = (xdt * dout_w).astype(BF16)
    hg = SSM_HEADS // N_GROUPS
    gw = hg * SSM_HEAD_DIM
    for g in range(N_GROUPS):
        cg = cm[:, g * D_STATE:(g + 1) * D_STATE].astype(BF16)
        bg = bm[:, g * D_STATE:(g + 1) * D_STATE].astype(BF16)
        cb = _dot_nt(cg, bg)
        gs = slice(g * gw, (g + 1) * gw)
        prev = state_ref[:, gs]
        y_ref[:, gs] = _dot(cg, prev.astype(BF16)) * ecs_w[:, gs]
        for hh in range(hg):
            h = g * hg + hh
            seg = cs[:, h:h + 1] - cs_t[h:h + 1, :]
            lm = jnp.exp(jnp.where(causal, seg, -jnp.inf))
            hs = slice(h * SSM_HEAD_DIM, (h + 1) * SSM_HEAD_DIM)
            y_ref[:, hs] += _dot((cb * lm).astype(BF16), xdt_b[:, hs])
        state_ref[:, gs] = prev * cdec_w[:, gs] + _dot_tn(bg, xdec_b[:, gs])

    y = y_ref[...] + dsk_ref[...] * xs
    y = y * _silu(z_ref[...])
    for g in range(N_GROUPS):
        gs = slice(g * gw, (g + 1) * gw)
        yg = y[:, gs]
        inv = lax.rsqrt(jnp.mean(yg * yg, axis=-1, keepdims=True) + NORM_EPS)
        o_ref[:, gs] = (yg * inv * nrm_ref[:, gs]).astype(o_ref.dtype)


def ssd_mixer(proj, conv_w, conv_b, dt_bias, a_log, d_skip, ssd_norm, *, batch, seq):
    nc = seq // CHUNK
    pad = LANES - SSM_HEADS
    cwx, cwb, cwc = conv_w[:, :D_INNER], conv_w[:, D_INNER:D_INNER + BC_W], conv_w[:, D_INNER + BC_W:]
    cbx, cbb, cbc = (conv_b[None, :D_INNER], conv_b[None, D_INNER:D_INNER + BC_W], conv_b[None, D_INNER + BC_W:])
    dtb = jnp.pad(dt_bias, (0, pad))[None, :]
    alog = jnp.pad(a_log, (0, pad))[None, :]
    dsk = jnp.repeat(d_skip, SSM_HEAD_DIM)[None, :]
    head_of_channel = jnp.arange(D_INNER) // SSM_HEAD_DIM
    expand = (jnp.arange(LANES)[:, None] == head_of_channel[None, :]).astype(F32)

    def row(b, c):
        return b * nc + c

    def full(shape):
        return pl.BlockSpec(shape, lambda b, c: (0, 0))

    hist = SUBLANES
    return pl.pallas_call(
        _ssd_kernel,
        out_shape=jax.ShapeDtypeStruct((batch * seq, D_INNER), BF16),
        grid=(batch, nc),
        in_specs=[pl.BlockSpec((CHUNK, D_INNER), lambda b, c: (row(b, c), 1)),
                  pl.BlockSpec((CHUNK, D_INNER), lambda b, c: (row(b, c), 2)),
                  pl.BlockSpec((CHUNK, BC_W), lambda b, c: (row(b, c), 14)),
                  pl.BlockSpec((CHUNK, BC_W), lambda b, c: (row(b, c), 15)),
                  pl.BlockSpec((CHUNK, LANES), lambda b, c: (row(b, c), 32)),
                  full((CONV_WIDTH, D_INNER)), full((CONV_WIDTH, BC_W)), full((CONV_WIDTH, BC_W)),
                  full((1, D_INNER)), full((1, BC_W)), full((1, BC_W)),
                  full((1, LANES)), full((1, LANES)), full((1, D_INNER)), full((1, D_INNER)),
                  full((LANES, D_INNER))],
        out_specs=pl.BlockSpec((CHUNK, D_INNER), lambda b, c: (row(b, c), 0)),
        scratch_shapes=[pltpu.VMEM((CHUNK + hist, D_INNER), F32),
                        pltpu.VMEM((CHUNK + hist, BC_W), F32),
                        pltpu.VMEM((CHUNK + hist, BC_W), F32),
                        pltpu.VMEM((D_STATE, D_INNER), F32),
                        pltpu.VMEM((CHUNK, D_INNER), F32)],
        compiler_params=_cparams("parallel", "arbitrary"),
        name="ssd_mixer",
    )(proj, proj, proj, proj, proj, cwx, cwb, cwc, cbx, cbb, cbc, dtb, alog, dsk, ssd_norm[None, :], expand)


def _rwkv_mix_kernel(x_ref, xp_ref, g_ref, mix_ref, o_ref, hp_ref, *, tm, seq):
    i = pl.program_id(0)
    hist = SUBLANES
    g = g_ref[...]
    h = _rms(x_ref[...], g)
    keep = jnp.where((i * tm) % seq == 0, 0.0, 1.0)
    hp_ref[0:hist, :] = _rms(xp_ref[...], g) * keep
    hp_ref[hist:hist + tm, :] = h
    xx = hp_ref[hist - 1:hist - 1 + tm, :] - h
    for m in range(6):
        o_ref[m] = (h + xx * mix_ref[m:m + 1, :]).astype(o_ref.dtype)


def rwkv_mix(x, g, mix, *, tm, seq):
    t, d = x.shape
    hist = SUBLANES
    return pl.pallas_call(
        functools.partial(_rwkv_mix_kernel, tm=tm, seq=seq),
        out_shape=jax.ShapeDtypeStruct((6, t, d), BF16),
        grid=(t // tm,),
        in_specs=[pl.BlockSpec((tm, d), lambda i: (i, 0)),
                  pl.BlockSpec((hist, d), lambda i: (jnp.maximum(i * (tm // hist) - 1, 0), 0)),
                  pl.BlockSpec((1, d), lambda i: (0, 0)),
                  pl.BlockSpec((6, d), lambda i: (0, 0))],
        out_specs=pl.BlockSpec((6, tm, d), lambda i: (0, i, 0)),
        scratch_shapes=[pltpu.VMEM((tm + hist, d), F32)],
        compiler_params=_cparams("parallel"),
        name="rwkv_mix",
    )(x, x, g, mix)


def _bmm_kernel(x_ref, w_ref, o_ref):
    o_ref[...] = _dot(x_ref[...], w_ref[...])


def batched_matmul(xs, ws, *, tm, tn):
    nm, d, n = ws.shape
    t = xs.shape[1]
    return pl.pallas_call(
        _bmm_kernel,
        out_shape=jax.ShapeDtypeStruct((nm, t, n), F32),
        grid=(nm, t // tm, n // tn),
        in_specs=[pl.BlockSpec((None, tm, d), lambda m, i, j: (m, i, 0)),
                  pl.BlockSpec((None, d, tn), lambda m, i, j: (m, 0, j))],
        out_specs=pl.BlockSpec((None, tm, tn), lambda m, i, j: (m, i, j)),
        compiler_params=_cparams("parallel", "parallel", "arbitrary"),
        name="batched_matmul",
    )(xs, ws)


def _lora_kernel(xw_ref, xa_ref, xg_ref, w1_ref, a1_ref, g1_ref, w2_ref, a2_ref, g2_ref, w0_ref, a0_ref,
                 ld_ref, a_ref, g_ref):
    tw = jnp.tanh(_dot(xw_ref[...], w1_ref[...])).astype(BF16)
    zw = w0_ref[...] + _dot(tw, w2_ref[...])
    logw = -_softplus(-zw) - 0.5
    ld_ref[...] = -jnp.exp(logw)
    ta = _dot(xa_ref[...], a1_ref[...]).astype(BF16)
    a_ref[...] = _sigmoid(a0_ref[...] + _dot(ta, a2_ref[...]))
    tg = _sigmoid(_dot(xg_ref[...], g1_ref[...])).astype(BF16)
    g_ref[...] = _dot(tg, g2_ref[...])


def rwkv_lora(xs, w1, a1, g1, w2, a2, g2, w0, a0, *, tm):
    _, t, d = xs.shape

    def full(a):
        return pl.BlockSpec(a.shape, lambda i: (0, 0))

    out = jax.ShapeDtypeStruct((t, d), F32)
    return pl.pallas_call(
        _lora_kernel,
        out_shape=(out, out, out),
        grid=(t // tm,),
        in_specs=[pl.BlockSpec((None, tm, d), lambda i: (3, i, 0)),
                  pl.BlockSpec((None, tm, d), lambda i: (4, i, 0)),
                  pl.BlockSpec((None, tm, d), lambda i: (5, i, 0)),
                  full(w1), full(a1), full(g1), full(w2), full(a2), full(g2), full(w0), full(a0)],
        out_specs=(pl.BlockSpec((tm, d), lambda i: (i, 0)),) * 3,
        compiler_params=_cparams("parallel"),
        name="rwkv_lora",
    )(xs, xs, xs, w1, a1, g1, w2, a2, g2, w0, a0)


def _rwkv_scan_kernel(r_ref, k_ref, v_ref, ld_ref, a_ref, g_ref, kk_ref, ka_ref, rk_ref, lnw_ref, lnb_ref,
                      o_ref, s_ref, *, heads, chunks):
    c = RWKV_CHUNK
    n = HEAD_DIM

    @pl.when(pl.program_id(2) == 0)
    def _():
        s_ref[...] = jnp.zeros_like(s_ref)

    ri = lax.broadcasted_iota(jnp.int32, (c, c), 0)
    ci = lax.broadcasted_iota(jnp.int32, (c, c), 1)
    tril = jnp.where(ri >= ci, 1.0, 0.0).astype(F32)
    ri2 = lax.broadcasted_iota(jnp.int32, (2 * c, 2 * c), 0)
    ci2 = lax.broadcasted_iota(jnp.int32, (2 * c, 2 * c), 1) % c
    gmask = ci2 <= jnp.where(ri2 < c, ri2 - 1, ri2 - c)
    zeros_cn = jnp.zeros((c, n), F32)
    n_steps = c.bit_length() - 1

    for ch in range(chunks):
        rows = slice(ch * c, (ch + 1) * c)
        ld = ld_ref[rows, :]
        cl = jnp.dot(tril, ld, precision=HIGHEST, preferred_element_type=F32)
        cl_last = cl[c - 1:c, :]
        e_inc = jnp.exp(cl)
        e_neg = jnp.exp(-cl)
        e_exc = jnp.exp(cl - ld)
        e_bar = jnp.exp(cl_last - cl)
        w_all = jnp.exp(cl_last)
        r_all, k_all, v_all, a_all = r_ref[rows, :], k_ref[rows, :], v_ref[rows, :], a_ref[rows, :]
        kk_all = k_all * kk_ref[...]
        kf_all = k_all * (1.0 + (a_all - 1.0) * ka_ref[...])
        rkk_all = r_all * kf_all * rk_ref[...]
        for h in range(heads):
            hs = slice(h * n, (h + 1) * n)
            kk = kk_all[:, hs]
            kk = kk * lax.rsqrt(jnp.maximum(jnp.sum(kk * kk, axis=-1, keepdims=True), 1e-24))
            bb = kk * a_all[:, hs]
            kf, r, v = kf_all[:, hs], r_all[:, hs], v_all[:, hs]
            a_t = -kk * e_exc[:, hs]
            r_t = r * e_inc[:, hs]
            b_t = bb * e_neg[:, hs]
            k_t = kf * e_neg[:, hs]
            v_b = v.astype(BF16)
            gm = _dot_nt(jnp.concatenate([a_t, r_t], axis=0).astype(BF16),
                         jnp.concatenate([b_t, k_t], axis=0).astype(BF16))
            gm = jnp.where(gmask, gm, 0.0)
            p = gm[0:c, 0:c]
            x = jnp.concatenate([a_t, _dot(gm[0:c, c:2 * c].astype(BF16), v_b)], axis=1)
            for step in range(n_steps):
                p_b = p.astype(BF16)
                x = x + _dot(p_b, x.astype(BF16))
                if step + 1 < n_steps:
                    p = _dot(p_b, p_b)
            rhs = jnp.concatenate([x, jnp.concatenate([zeros_cn, v], axis=1)], axis=0).astype(BF16)
            top = _dot(gm[c:2 * c, :].astype(BF16), rhs)
            r_hat = r_t + top[:, 0:n]
            s0 = s_ref[h]
            inter = _dot_nt(jnp.concatenate([x[:, 0:n], r_hat], axis=0).astype(BF16), s0.astype(BF16))
            u = inter[0:c] + x[:, n:2 * n]
            y = inter[c:2 * c] + top[:, n:2 * n]
            bk_bar = jnp.concatenate([bb * e_bar[:, hs], kf * e_bar[:, hs]], axis=0).astype(BF16)
            s_ref[h] = s0 * w_all[:, hs] + _dot_tn(jnp.concatenate([u, v], axis=0).astype(BF16), bk_bar)
            mu = jnp.mean(y, axis=-1, keepdims=True)
            yc = y - mu
            var = jnp.mean(yc * yc, axis=-1, keepdims=True)
            yn = yc * lax.rsqrt(var + GN_EPS) * lnw_ref[:, hs] + lnb_ref[:, hs]
            bonus = jnp.sum(rkk_all[:, hs], axis=-1, keepdims=True) * v
            o_ref[rows, hs] = ((yn + bonus) * g_ref[rows, hs]).astype(o_ref.dtype)


def rwkv_scan(rkv, ld, a, g, k_k, k_a, r_k, ln_w, ln_b, *, batch, seq, heads, chunks):
    t, d = ld.shape
    rows = chunks * RWKV_CHUNK
    nblk = seq // rows
    width = heads * HEAD_DIM

    def data(b, hg, c):
        return (b * nblk + c, hg)

    def stacked(m):
        return pl.BlockSpec((None, rows, width), lambda b, hg, c: (m, b * nblk + c, hg))

    def param():
        return pl.BlockSpec((1, width), lambda b, hg, c: (0, hg))

    return pl.pallas_call(
        functools.partial(_rwkv_scan_kernel, heads=heads, chunks=chunks),
        out_shape=jax.ShapeDtypeStruct((t, d), BF16),
        grid=(batch, d // width, nblk),
        in_specs=[stacked(0), stacked(1), stacked(2),
                  pl.BlockSpec((rows, width), data), pl.BlockSpec((rows, width), data),
                  pl.BlockSpec((rows, width), data),
                  param(), param(), param(), param(), param()],
        out_specs=pl.BlockSpec((rows, width), data),
        scratch_shapes=[pltpu.VMEM((heads, HEAD_DIM, HEAD_DIM), F32)],
        compiler_params=_cparams("parallel", "parallel", "arbitrary"),
        name="rwkv_scan",
    )(rkv, rkv, rkv, ld, a, g, k_k[None, :], k_a[None, :], r_k.reshape(1, d), ln_w[None, :], ln_b[None, :])


def _attn_ssd_layer(x, batch, seq, norm_g, w_in, conv_w, conv_b, dt_bias, a_log, d_skip, ssd_norm,
                    q_norm, k_norm, sinks, w_out, cos_t, sin_t):
    q_end, k_end, v_end, z_end, xbc_end = 1024, 1280, 1536, 2560, 4096
    x_end = z_end + D_INNER
    w_in_p = jnp.concatenate(
        [w_in[:, :q_end], w_in[:, v_end:z_end], w_in[:, z_end:x_end], w_in[:, q_end:k_end], w_in[:, k_end:v_end],
         w_in[:, x_end:xbc_end], w_in[:, xbc_end:], jnp.zeros((D_MODEL, LANES - SSM_HEADS), w_in.dtype)],
        axis=1).astype(BF16)
    proj = norm_matmul(x, norm_g[None, :], w_in_p, tm=512, tn=PROJ_W // 3)
    y_attn = swa_attention(proj, cos_t, sin_t, q_norm[None, :], k_norm[None, :], sinks, batch=batch, seq=seq)
    y_ssd = ssd_mixer(proj, conv_w, conv_b, dt_bias, a_log, d_skip, ssd_norm, batch=batch, seq=seq)
    w_out_b = w_out.astype(BF16)
    return matmul_residual(x, [y_attn, y_ssd], [w_out_b[:Q_W], w_out_b[Q_W:]], tm=512, tn=1024)


def _rwkv_layer(x, batch, seq, norm_g, mix, w0, w1, w2, a0, a1, a2, g1, g2, k_k, k_a, r_k,
                w_r, w_k, w_v, w_o, ln_w, ln_b):
    mix_p = jnp.stack([mix[0], mix[2], mix[3], mix[1], mix[4], mix[5]])
    xs = rwkv_mix(x, norm_g[None, :], mix_p, tm=256, seq=seq)
    rkv = batched_matmul(xs, jnp.stack([w_r, w_k, w_v]).astype(BF16), tm=512, tn=1024)
    lora_pad = LANES - w1.shape[1]
    w1_p = jnp.pad(w1, ((0, 0), (0, lora_pad))).astype(BF16)
    a1_p = jnp.pad(a1, ((0, 0), (0, lora_pad))).astype(BF16)
    w2_p = jnp.pad(w2, ((0, lora_pad), (0, 0))).astype(BF16)
    a2_p = jnp.pad(a2, ((0, lora_pad), (0, 0))).astype(BF16)
    ld, a, g = rwkv_lora(xs, w1_p, a1_p, g1.astype(BF16), w2_p, a2_p, g2.astype(BF16),
                         w0[None, :], a0[None, :], tm=256)
    yg = rwkv_scan(rkv, ld, a, g, k_k, k_a, r_k, ln_w, ln_b, batch=batch, seq=seq, heads=4, chunks=2)
    return matmul_residual(x, [yg], [w_o.astype(BF16)], tm=512, tn=1024)


def kernel(x, norm_mix, norm_ffn, w_up, w_down, ev_w_in, ev_conv_w, ev_conv_b, ev_dt_bias, ev_a_log, ev_d_skip, ev_ssd_norm, ev_q_norm, ev_k_norm, ev_sinks, ev_w_out, od_mix, od_w0, od_w1, od_w2, od_a0, od_a1, od_a2, od_g1, od_g2, od_k_k, od_k_a, od_r_k, od_w_r, od_w_k, od_w_v, od_w_o, od_ln_w, od_ln_b):
    batch, seq, d = x.shape
    depth = norm_mix.shape[0]
    inv_freq = ROPE_THETA ** (-jnp.arange(0, HEAD_DIM, 2, dtype=F32) / HEAD_DIM)
    ang = jnp.arange(seq, dtype=F32)[:, None] * inv_freq[None, :]
    cos, sin = jnp.cos(ang), jnp.sin(ang)
    cos_t = jnp.concatenate([cos, cos], axis=1)
    sin_t = jnp.concatenate([-sin, sin], axis=1)
    xf = x.reshape(batch * seq, d)
    for i in range(depth):
        j = i // 2
        if i % 2 == 0:
            xf = _attn_ssd_layer(xf, batch, seq, norm_mix[i], ev_w_in[j], ev_conv_w[j], ev_conv_b[j],
                                 ev_dt_bias[j], ev_a_log[j], ev_d_skip[j], ev_ssd_norm[j], ev_q_norm[j],
                                 ev_k_norm[j], ev_sinks[j], ev_w_out[j], cos_t, sin_t)
        else:
            xf = _rwkv_layer(xf, batch, seq, norm_mix[i], od_mix[j], od_w0[j], od_w1[j], od_w2[j], od_a0[j],
                             od_a1[j], od_a2[j], od_g1[j], od_g2[j], od_k_k[j], od_k_a[j], od_r_k[j],
                             od_w_r[j], od_w_k[j], od_w_v[j], od_w_o[j], od_ln_w[j], od_ln_b[j])
        xf = mlp_block(xf, norm_ffn[i][None, :], w_up[i].astype(BF16), w_down[i].astype(BF16), tm=512, tf=512)
    return xf.reshape(batch, seq, d)
```

```python
import functools

import jax
import jax.numpy as jnp
from jax import lax
from jax.experimental import pallas as pl
from jax.experimental.pallas import tpu as pltpu

F32 = jnp.float32
BF16 = jnp.bfloat16
HIGHEST = lax.Precision.HIGHEST

D_MODEL = 2048
HEAD_DIM = 64
N_Q_HEADS = 16
N_KV_HEADS = 4
Q_REP = N_Q_HEADS // N_KV_HEADS
WINDOW = 128
ROPE_THETA = 10000.0
SSM_HEADS = 16
SSM_HEAD_DIM = 64
D_INNER = 1024
D_STATE = 128
N_GROUPS = 2
CONV_WIDTH = 4
CHUNK = 128
RWKV_HEADS = 32
D_FF = 4 * D_MODEL
NORM_EPS = 1e-6
GN_EPS = 64e-5
Q_W = 1024
KV_W = 256
BC_W = 256

LANES = 128
SUBLANES = 8
VMEM_LIMIT_BYTES = 56 * 1024 * 1024

PROJ_W = 3 * 1024 + 4 * 256 + LANES
RWKV_CHUNK = 64


TM_PROJ = 1024
TM_OUT = 512
TM_BMM = 1024
TM_MLP = 1024
TF_MLP = 512


def _rows(x, want):
    return min(want, x.shape[0])


def _cparams(*sem):
    return pltpu.CompilerParams(dimension_semantics=sem, vmem_limit_bytes=VMEM_LIMIT_BYTES)


def _rms(x, g):
    inv = lax.rsqrt(jnp.mean(x * x, axis=-1, keepdims=True) + NORM_EPS)
    return x * inv * g


def _silu(x):
    return x * (1.0 / (1.0 + jnp.exp(-x)))


def _sigmoid(x):
    return 1.0 / (1.0 + jnp.exp(-x))


def _softplus(x):
    return jnp.maximum(x, 0.0) + jnp.log1p(jnp.exp(-jnp.abs(x)))


def _dot(a, b):
    return jnp.dot(a, b, preferred_element_type=F32)


def _dot_nt(a, b):
    return lax.dot_general(a, b, (((1,), (1,)), ((), ())), preferred_element_type=F32)


def _dot_tn(a, b):
    return lax.dot_general(a, b, (((0,), (0,)), ((), ())), preferred_element_type=F32)


def _norm_matmul_kernel(x_ref, g_ref, w_ref, o_ref, h_ref):
    @pl.when(pl.program_id(1) == 0)
    def _():
        h_ref[...] = _rms(x_ref[...], g_ref[...]).astype(BF16)

    o_ref[...] = _dot(h_ref[...], w_ref[...])


def norm_matmul(x, g, w, *, tm, tn):
    t, d = x.shape
    n = w.shape[1]
    return pl.pallas_call(
        _norm_matmul_kernel,
        out_shape=jax.ShapeDtypeStruct((t, n), F32),
        grid=(t // tm, n // tn),
        in_specs=[pl.BlockSpec((tm, d), lambda i, j: (i, 0)),
                  pl.BlockSpec((1, d), lambda i, j: (0, 0)),
                  pl.BlockSpec((d, tn), lambda i, j: (0, j))],
        out_specs=pl.BlockSpec((tm, tn), lambda i, j: (i, j)),
        scratch_shapes=[pltpu.VMEM((tm, d), BF16)],
        compiler_params=_cparams("parallel", "arbitrary"),
        name="norm_matmul",
    )(x, g, w)


def _matmul_residual_kernel(*refs, n_in):
    res_ref = refs[0]
    a_refs = refs[1:1 + n_in]
    w_refs = refs[1 + n_in:1 + 2 * n_in]
    o_ref = refs[1 + 2 * n_in]
    acc = res_ref[...]
    for a_ref, w_ref in zip(a_refs, w_refs):
        acc = acc + _dot(a_ref[...], w_ref[...])
    o_ref[...] = acc


def matmul_residual(res, a_list, w_list, *, tm, tn):
    t, n = res.shape
    n_in = len(a_list)
    in_specs = [pl.BlockSpec((tm, tn), lambda i, j: (i, j))]
    in_specs += [pl.BlockSpec((tm, a.shape[1]), lambda i, j: (i, 0)) for a in a_list]
    in_specs += [pl.BlockSpec((w.shape[0], tn), lambda i, j: (0, j)) for w in w_list]
    return pl.pallas_call(
        functools.partial(_matmul_residual_kernel, n_in=n_in),
        out_shape=jax.ShapeDtypeStruct((t, n), F32),
        grid=(t // tm, n // tn),
        in_specs=in_specs,
        out_specs=pl.BlockSpec((tm, tn), lambda i, j: (i, j)),
        compiler_params=_cparams("parallel", "arbitrary"),
        name="matmul_residual",
    )(res, *a_list, *w_list)


def _mlp_kernel(x_ref, g_ref, wu_ref, wd_ref, o_ref, h_ref):
    @pl.when(pl.program_id(1) == 0)
    def _():
        x = x_ref[...]
        h_ref[...] = _rms(x, g_ref[...]).astype(BF16)
        o_ref[...] = x

    u = _dot(h_ref[...], wu_ref[...])
    a = jnp.square(jnp.maximum(u, 0.0)).astype(BF16)
    o_ref[...] += _dot(a, wd_ref[...])


def mlp_block(x, g, w_up, w_down, *, tm, tf):
    t, d = x.shape
    f = w_up.shape[1]
    return pl.pallas_call(
        _mlp_kernel,
        out_shape=jax.ShapeDtypeStruct((t, d), F32),
        grid=(t // tm, f // tf),
        in_specs=[pl.BlockSpec((tm, d), lambda i, j: (i, 0)),
                  pl.BlockSpec((1, d), lambda i, j: (0, 0)),
                  pl.BlockSpec((d, tf), lambda i, j: (0, j)),
                  pl.BlockSpec((tf, d), lambda i, j: (j, 0))],
        out_specs=pl.BlockSpec((tm, d), lambda i, j: (i, 0)),
        scratch_shapes=[pltpu.VMEM((tm, d), BF16)],
        compiler_params=_cparams("parallel", "arbitrary"),
        name="mlp_block",
    )(x, g, w_up, w_down)


def _low_lanes(shape):
    return lax.broadcasted_iota(jnp.int32, shape, 1) % LANES < HEAD_DIM


def _block_rows(arr):
    low = _low_lanes(arr.shape)
    zero = jnp.zeros_like(arr)
    return jnp.concatenate([jnp.where(low, arr, zero), jnp.where(low, zero, arr)], axis=0)


def _head_sums(blk):
    low = _low_lanes(blk.shape)
    s_lo = jnp.sum(jnp.where(low, blk, 0.0), axis=-1, keepdims=True)
    s_hi = jnp.sum(jnp.where(low, 0.0, blk), axis=-1, keepdims=True)
    return jnp.where(low, s_lo, s_hi)


def _both_halves(pair):
    swapped = pltpu.roll(pair, HEAD_DIM, axis=1)
    low = _low_lanes(pair.shape)
    return jnp.where(low, pair, swapped), jnp.where(low, swapped, pair)


def _norm_rope(xp, gain, cos, sin):
    half = HEAD_DIM // 2
    ms = _head_sums(xp * xp) * (1.0 / HEAD_DIM)
    xn = xp * lax.rsqrt(ms + NORM_EPS) * gain
    from_below = pltpu.roll(xn, half, axis=1)
    from_above = pltpu.roll(xn, LANES - half, axis=1)
    first_half = lax.broadcasted_iota(jnp.int32, xp.shape, 1) % HEAD_DIM < half
    return xn * cos + jnp.where(first_half, from_above, from_below) * sin


def _attn_kernel(sinks_ref, q_ref, kc_ref, kp_ref, vc_ref, vp_ref, cc_ref, sc_ref, cp_ref, sp_ref,
                 qn_ref, kn_ref, o_ref):
    n = pl.program_id(1)
    w = WINDOW
    cos_c, sin_c = cc_ref[...], sc_ref[...]
    cos_pc = jnp.concatenate([cp_ref[...], cos_c], axis=0)
    sin_pc = jnp.concatenate([sp_ref[...], sin_c], axis=0)
    qn, kn = qn_ref[...], kn_ref[...]

    rows = lax.broadcasted_iota(jnp.int32, (Q_REP * w, 2 * w), 0)
    qi = rows % w
    kj = lax.broadcasted_iota(jnp.int32, (Q_REP * w, 2 * w), 1)
    lo = jnp.where(n == 0, w - 1, 0)
    mask = (kj > qi) & (kj <= qi + w) & (kj > lo)

    k_dup, v_dup = [], []
    for j in range(N_KV_HEADS // 2):
        ps = slice(j * LANES, (j + 1) * LANES)
        k_pair = _norm_rope(jnp.concatenate([kp_ref[:, ps], kc_ref[:, ps]], axis=0), kn, cos_pc, sin_pc)
        k_dup += [t.astype(BF16) for t in _both_halves(k_pair)]
        v_dup += [t.astype(BF16) for t in _both_halves(jnp.concatenate([vp_ref[:, ps], vc_ref[:, ps]], axis=0))]
    q_rows = []
    for i in range(N_Q_HEADS // 2):
        qp = _norm_rope(q_ref[:, i * LANES:(i + 1) * LANES], qn, cos_c, sin_c) * (HEAD_DIM ** -0.5)
        q_rows.append(_block_rows(qp.astype(BF16)))
    s = [_dot_nt(jnp.concatenate(q_rows[2 * g:2 * g + 2], axis=0), k_dup[g]) for g in range(N_KV_HEADS)]
    p, denom = [], []
    for g in range(N_KV_HEADS):
        sink = jnp.concatenate([jnp.full((w, 1), sinks_ref[g * Q_REP + r], F32) for r in range(Q_REP)], axis=0)
        sg = jnp.where(mask, s[g], -jnp.inf)
        m = jnp.maximum(jnp.max(sg, axis=-1, keepdims=True), sink)
        e = jnp.exp(sg - m)
        denom.append(jnp.sum(e, axis=-1, keepdims=True) + jnp.exp(sink - m))
        p.append(e.astype(BF16))
    o = [_dot(p[g], v_dup[g]) / denom[g] for g in range(N_KV_HEADS)]
    low = _low_lanes((w, LANES))
    for i in range(N_Q_HEADS // 2):
        og = o[i // 2]
        r0 = (i % 2) * 2 * w
        o_ref[:, i * LANES:(i + 1) * LANES] = jnp.where(low, og[r0:r0 + w], og[r0 + w:r0 + 2 * w]).astype(o_ref.dtype)


def swa_attention(proj, cos_t, sin_t, q_norm, k_norm, sinks, *, batch, seq):
    nb = seq // WINDOW
    kcol = 3 * 1024 // KV_W
    vcol = kcol + 1

    def cur(b, n):
        return b * nb + n

    def prev(b, n):
        return b * nb + jnp.maximum(n - 1, 0)

    return pl.pallas_call(
        _attn_kernel,
        out_shape=jax.ShapeDtypeStruct((batch * seq, Q_W), BF16),
        grid=(batch, nb),
        in_specs=[pl.BlockSpec(memory_space=pltpu.SMEM),
                  pl.BlockSpec((WINDOW, Q_W), lambda b, n: (cur(b, n), 0)),
                  pl.BlockSpec((WINDOW, KV_W), lambda b, n: (cur(b, n), kcol)),
                  pl.BlockSpec((WINDOW, KV_W), lambda b, n: (prev(b, n), kcol)),
                  pl.BlockSpec((WINDOW, KV_W), lambda b, n: (cur(b, n), vcol)),
                  pl.BlockSpec((WINDOW, KV_W), lambda b, n: (prev(b, n), vcol)),
                  pl.BlockSpec((WINDOW, LANES), lambda b, n: (n, 0)),
                  pl.BlockSpec((WINDOW, LANES), lambda b, n: (n, 0)),
                  pl.BlockSpec((WINDOW, LANES), lambda b, n: (jnp.maximum(n - 1, 0), 0)),
                  pl.BlockSpec((WINDOW, LANES), lambda b, n: (jnp.maximum(n - 1, 0), 0)),
                  pl.BlockSpec((1, LANES), lambda b, n: (0, 0)),
                  pl.BlockSpec((1, LANES), lambda b, n: (0, 0))],
        out_specs=pl.BlockSpec((WINDOW, Q_W), lambda b, n: (cur(b, n), 0)),
        compiler_params=_cparams("parallel", "parallel"),
        name="swa_attention",
    )(sinks, proj, proj, proj, proj, proj, cos_t, sin_t, cos_t, sin_t, q_norm, k_norm)


def _ssd_kernel(z_ref, x_ref, b_ref, c_ref, dt_ref, cwx_ref, cwb_ref, cwc_ref, cbx_ref, cbb_ref, cbc_ref,
                dtb_ref, alog_ref, dsk_ref, nrm_ref, exp_ref, o_ref,
                xpx_ref, xpb_ref, xpc_ref, state_ref, y_ref):
    q = CHUNK
    hist = SUBLANES

    @pl.when(pl.program_id(1) == 0)
    def _():
        xpx_ref[0:hist, :] = jnp.zeros((hist, D_INNER), F32)
        xpb_ref[0:hist, :] = jnp.zeros((hist, BC_W), F32)
        xpc_ref[0:hist, :] = jnp.zeros((hist, BC_W), F32)
        state_ref[...] = jnp.zeros_like(state_ref)

    def conv(raw_ref, pad_ref, cw_ref, cb_ref):
        pad_ref[hist:hist + q, :] = raw_ref[...]
        acc = cb_ref[...]
        for j in range(CONV_WIDTH):
            start = hist - (CONV_WIDTH - 1) + j
            acc = acc + pad_ref[start:start + q, :] * cw_ref[j:j + 1, :]
        pad_ref[0:hist, :] = pad_ref[q:q + hist, :]
        return _silu(acc)

    xs = conv(x_ref, xpx_ref, cwx_ref, cbx_ref)
    bm = conv(b_ref, xpb_ref, cwb_ref, cbb_ref)
    cm = conv(c_ref, xpc_ref, cwc_ref, cbc_ref)

    dt = _softplus(dt_ref[...] + dtb_ref[...])
    dta = dt * (-jnp.exp(alog_ref[...]))
    ri = lax.broadcasted_iota(jnp.int32, (q, q), 0)
    ci = lax.broadcasted_iota(jnp.int32, (q, q), 1)
    causal = ri >= ci
    tril = jnp.where(causal, 1.0, 0.0).astype(F32)
    cs = jnp.dot(tril, dta, precision=HIGHEST, preferred_element_type=F32)
    cs_t = cs.T
    cs_last = cs[q - 1:q, :]
    expand = exp_ref[...]
    stacked = jnp.concatenate([dt, jnp.exp(cs), jnp.exp(cs_last - cs)], axis=0)
    wide = jnp.dot(stacked, expand, precision=HIGHEST, preferred_element_type=F32)
    dt_w, ecs_w, dout_w = wide[0:q], wide[q:2 * q], wide[2 * q:3 * q]
    cdec_w = ecs_w[q - 1:q, :]

    xdt = xs * dt_w
    xdt_b = xdt.astype(BF16)
    xdec_b = (xdt * dout_w).astype(BF16)
    hg = SSM_HEADS // N_GROUPS
    gw = hg * SSM_HEAD_DIM
    for g in range(N_GROUPS):
        cg = cm[:, g * D_STATE:(g + 1) * D_STATE].astype(BF16)
        bg = bm[:, g * D_STATE:(g + 1) * D_STATE].astype(BF16)
        cb = _dot_nt(cg, bg)
        gs = slice(g * gw, (g + 1) * gw)
        prev = state_ref[:, gs]
        y_ref[:, gs] = _dot(cg, prev.astype(BF16)) * ecs_w[:, gs]
        for hh in range(hg):
            h = g * hg + hh
            seg = cs[:, h:h + 1] - cs_t[h:h + 1, :]
            lm = jnp.exp(jnp.where(causal, seg, -jnp.inf))
            hs = slice(h * SSM_HEAD_DIM, (h + 1) * SSM_HEAD_DIM)
            y_ref[:, hs] += _dot((cb * lm).astype(BF16), xdt_b[:, hs])
        state_ref[:, gs] = prev * cdec_w[:, gs] + _dot_tn(bg, xdec_b[:, gs])

    y = y_ref[...] + dsk_ref[...] * xs
    y = y * _silu(z_ref[...])
    for g in range(N_GROUPS):
        gs = slice(g * gw, (g + 1) * gw)
        yg = y[:, gs]
        inv = lax.rsqrt(jnp.mean(yg * yg, axis=-1, keepdims=True) + NORM_EPS)
        o_ref[:, gs] = (yg * inv * nrm_ref[:, gs]).astype(o_ref.dtype)


def ssd_mixer(proj, conv_w, conv_b, dt_bias, a_log, d_skip, ssd_norm, *, batch, seq):
    nc = seq // CHUNK
    pad = LANES - SSM_HEADS
    cwx, cwb, cwc = conv_w[:, :D_INNER], conv_w[:, D_INNER:D_INNER + BC_W], conv_w[:, D_INNER + BC_W:]
    cbx, cbb, cbc = (conv_b[None, :D_INNER], conv_b[None, D_INNER:D_INNER + BC_W], conv_b[None, D_INNER + BC_W:])
    dtb = jnp.pad(dt_bias, (0, pad))[None, :]
    alog = jnp.pad(a_log, (0, pad))[None, :]
    dsk = jnp.repeat(d_skip, SSM_HEAD_DIM)[None, :]
    head_of_channel = jnp.arange(D_INNER) // SSM_HEAD_DIM
    expand = (jnp.arange(LANES)[:, None] == head_of_channel[None, :]).astype(F32)

    def row(b, c):
        return b * nc + c

    def full(shape):
        return pl.BlockSpec(shape, lambda b, c: (0, 0))

    hist = SUBLANES
    return pl.pallas_call(
        _ssd_kernel,
        out_shape=jax.ShapeDtypeStruct((batch * seq, D_INNER), BF16),
        grid=(batch, nc),
        in_specs=[pl.BlockSpec((CHUNK, D_INNER), lambda b, c: (row(b, c), 1)),
                  pl.BlockSpec((CHUNK, D_INNER), lambda b, c: (row(b, c), 2)),
                  pl.BlockSpec((CHUNK, BC_W), lambda b, c: (row(b, c), 14)),
                  pl.BlockSpec((CHUNK, BC_W), lambda b, c: (row(b, c), 15)),
                  pl.BlockSpec((CHUNK, LANES), lambda b, c: (row(b, c), 32)),
                  full((CONV_WIDTH, D_INNER)), full((CONV_WIDTH, BC_W)), full((CONV_WIDTH, BC_W)),
                  full((1, D_INNER)), full((1, BC_W)), full((1, BC_W)),
                  full((1, LANES)), full((1, LANES)), full((1, D_INNER)), full((1, D_INNER)),
                  full((LANES, D_INNER))],
        out_specs=pl.BlockSpec((CHUNK, D_INNER), lambda b, c: (row(b, c), 0)),
        scratch_shapes=[pltpu.VMEM((CHUNK + hist, D_INNER), F32),
                        pltpu.VMEM((CHUNK + hist, BC_W), F32),
                        pltpu.VMEM((CHUNK + hist, BC_W), F32),
                        pltpu.VMEM((D_STATE, D_INNER), F32),
                        pltpu.VMEM((CHUNK, D_INNER), F32)],
        compiler_params=_cparams("parallel", "arbitrary"),
        name="ssd_mixer",
    )(proj, proj, proj, proj, proj, cwx, cwb, cwc, cbx, cbb, cbc, dtb, alog, dsk, ssd_norm[None, :], expand)


def _rwkv_mix_kernel(x_ref, xp_ref, g_ref, mix_ref, o_ref, hp_ref, *, tm, seq):
    i = pl.program_id(0)
    hist = SUBLANES
    g = g_ref[...]
    h = _rms(x_ref[...], g)
    keep = jnp.where((i * tm) % seq == 0, 0.0, 1.0)
    hp_ref[0:hist, :] = _rms(xp_ref[...], g) * keep
    hp_ref[hist:hist + tm, :] = h
    xx = hp_ref[hist - 1:hist - 1 + tm, :] - h
    for m in range(6):
        o_ref[m] = (h + xx * mix_ref[m:m + 1, :]).astype(o_ref.dtype)


def rwkv_mix(x, g, mix, *, tm, seq):
    t, d = x.shape
    hist = SUBLANES
    return pl.pallas_call(
        functools.partial(_rwkv_mix_kernel, tm=tm, seq=seq),
        out_shape=jax.ShapeDtypeStruct((6, t, d), BF16),
        grid=(t // tm,),
        in_specs=[pl.BlockSpec((tm, d), lambda i: (i, 0)),
                  pl.BlockSpec((hist, d), lambda i: (jnp.maximum(i * (tm // hist) - 1, 0), 0)),
                  pl.BlockSpec((1, d), lambda i: (0, 0)),
                  pl.BlockSpec((6, d), lambda i: (0, 0))],
        out_specs=pl.BlockSpec((6, tm, d), lambda i: (0, i, 0)),
        scratch_shapes=[pltpu.VMEM((tm + hist, d), F32)],
        compiler_params=_cparams("parallel"),
        name="rwkv_mix",
    )(x, x, g, mix)


def _bmm_kernel(x_ref, w_ref, o_ref):
    o_ref[...] = _dot(x_ref[...], w_ref[...])


def batched_matmul(xs, ws, *, tm, tn):
    nm, d, n = ws.shape
    t = xs.shape[1]
    return pl.pallas_call(
        _bmm_kernel,
        out_shape=jax.ShapeDtypeStruct((nm, t, n), F32),
        grid=(nm, t // tm, n // tn),
        in_specs=[pl.BlockSpec((None, tm, d), lambda m, i, j: (m, i, 0)),
                  pl.BlockSpec((None, d, tn), lambda m, i, j: (m, 0, j))],
        out_specs=pl.BlockSpec((None, tm, tn), lambda m, i, j: (m, i, j)),
        compiler_params=_cparams("parallel", "parallel", "arbitrary"),
        name="batched_matmul",
    )(xs, ws)


def _lora_kernel(xw_ref, xa_ref, xg_ref, w1_ref, a1_ref, g1_ref, w2_ref, a2_ref, g2_ref, w0_ref, a0_ref,
                 ld_ref, a_ref, g_ref):
    tw = jnp.tanh(_dot(xw_ref[...], w1_ref[...])).astype(BF16)
    zw = w0_ref[...] + _dot(tw, w2_ref[...])
    logw = -_softplus(-zw) - 0.5
    ld_ref[...] = -jnp.exp(logw)
    ta = _dot(xa_ref[...], a1_ref[...]).astype(BF16)
    a_ref[...] = _sigmoid(a0_ref[...] + _dot(ta, a2_ref[...]))
    tg = _sigmoid(_dot(xg_ref[...], g1_ref[...])).astype(BF16)
    g_ref[...] = _dot(tg, g2_ref[...])


def rwkv_lora(xs, w1, a1, g1, w2, a2, g2, w0, a0, *, tm):
    _, t, d = xs.shape

    def full(a):
        return pl.BlockSpec(a.shape, lambda i: (0, 0))

    out = jax.ShapeDtypeStruct((t, d), F32)
    return pl.pallas_call(
        _lora_kernel,
        out_shape=(out, out, out),
        grid=(t // tm,),
        in_specs=[pl.BlockSpec((None, tm, d), lambda i: (3, i, 0)),
                  pl.BlockSpec((None, tm, d), lambda i: (4, i, 0)),
                  pl.BlockSpec((None, tm, d), lambda i: (5, i, 0)),
                  full(w1), full(a1), full(g1), full(w2), full(a2), full(g2), full(w0), full(a0)],
        out_specs=(pl.BlockSpec((tm, d), lambda i: (i, 0)),) * 3,
        compiler_params=_cparams("parallel"),
        name="rwkv_lora",
    )(xs, xs, xs, w1, a1, g1, w2, a2, g2, w0, a0)


def _rwkv_scan_kernel(r_ref, k_ref, v_ref, ld_ref, a_ref, g_ref, kk_ref, ka_ref, rk_ref, lnw_ref, lnb_ref,
                      o_ref, s_ref, *, heads, chunks):
    c = RWKV_CHUNK
    n = HEAD_DIM

    @pl.when(pl.program_id(2) == 0)
    def _():
        s_ref[...] = jnp.zeros_like(s_ref)

    ri = lax.broadcasted_iota(jnp.int32, (c, c), 0)
    ci = lax.broadcasted_iota(jnp.int32, (c, c), 1)
    tril = jnp.where(ri >= ci, 1.0, 0.0).astype(F32)
    ri2 = lax.broadcasted_iota(jnp.int32, (2 * c, 4 * c), 0)
    ci2 = lax.broadcasted_iota(jnp.int32, (2 * c, 4 * c), 1) % c
    gmask = ci2 <= jnp.where(ri2 < c, ri2 - 1, ri2 - c)
    zeros_sq = jnp.zeros((LANES, LANES), BF16)
    n_steps = c.bit_length() - 1
    pairs = heads // 2

    block_rows, head_sums = _block_rows, _head_sums

    def all_head_sums(arr):
        return jnp.concatenate([head_sums(arr[:, j * LANES:(j + 1) * LANES]) for j in range(pairs)], axis=1)

    block_diag = (lax.broadcasted_iota(jnp.int32, (LANES, LANES), 0) < n) == _low_lanes((LANES, LANES))

    items = [(ch, j) for ch in range(chunks) for j in range(pairs)]
    at_p, rt_p, v_p, bd_v, bk_t, w_t, bonus, lhs1, rhs1 = ({} for _ in range(9))
    for ch in range(chunks):
        rows = slice(ch * c, (ch + 1) * c)
        ld = ld_ref[rows, :]
        cl = jnp.dot(tril, ld, precision=HIGHEST, preferred_element_type=F32)
        cl_last = cl[c - 1:c, :]
        e_inc = jnp.exp(cl)
        e_neg = jnp.exp(-cl)
        e_exc = jnp.exp(cl - ld)
        e_bar = jnp.exp(cl_last - cl)
        w_all = jnp.exp(cl_last)
        r_all, k_all, v_all, a_all = r_ref[rows, :], k_ref[rows, :], v_ref[rows, :], a_ref[rows, :]
        kk_all = k_all * kk_ref[...]
        kk_all = kk_all * lax.rsqrt(jnp.maximum(all_head_sums(kk_all * kk_all), 1e-24))
        kf_all = k_all * (1.0 + (a_all - 1.0) * ka_ref[...])
        bb_all = kk_all * a_all
        bonus[ch] = all_head_sums(r_all * kf_all * rk_ref[...]) * v_all
        at_all = (-kk_all * e_exc).astype(BF16)
        rt_all = (r_all * e_inc).astype(BF16)
        bt_all = (bb_all * e_neg).astype(BF16)
        kt_all = (kf_all * e_neg).astype(BF16)
        vb_all = v_all.astype(BF16)
        bbar_all = bb_all * e_bar
        kbar_all = kf_all * e_bar
        w_rows = jnp.broadcast_to(w_all, (2 * c, heads * n))
        for j in range(pairs):
            it = (ch, j)
            ps = slice(j * LANES, (j + 1) * LANES)
            at_p[it], rt_p[it], v_p[it] = at_all[:, ps], rt_all[:, ps], vb_all[:, ps]
            lhs1[it] = jnp.concatenate([at_p[it], rt_p[it]], axis=0)
            rhs1[it] = jnp.concatenate([block_rows(bt_all[:, ps]), block_rows(kt_all[:, ps])], axis=0)
            bd_v[it] = block_rows(v_p[it])
            bk_t[it] = jnp.concatenate([bbar_all[:, ps], kbar_all[:, ps]], axis=0).T.astype(BF16)
            w_t[it] = w_rows[:, ps].T

    gm = {it: jnp.where(gmask, _dot_nt(lhs1[it], rhs1[it]), 0.0) for it in items}
    a_r = {it: gm[it][c:2 * c, :].astype(BF16) for it in items}
    p_b = {it: gm[it][0:c, 0:LANES].astype(BF16) for it in items}
    akv = {it: _dot(gm[it][0:c, LANES:2 * LANES].astype(BF16), bd_v[it]) for it in items}
    x = {it: jnp.concatenate([at_p[it].astype(F32), akv[it]], axis=1) for it in items}
    for step in range(n_steps):
        x = {it: x[it] + _dot(p_b[it], block_rows(x[it].astype(BF16))) for it in items}
        if step + 1 < n_steps:
            p_b = {it: _dot(p_b[it], block_rows(p_b[it])).astype(BF16) for it in items}
    top = {}
    for it in items:
        rhs = jnp.concatenate([block_rows(x[it].astype(BF16)),
                               jnp.concatenate([zeros_sq, bd_v[it]], axis=1)], axis=0)
        top[it] = _dot(a_r[it], rhs)
    lhs5 = {it: jnp.concatenate([x[it][:, 0:LANES], rt_p[it].astype(F32) + top[it][:, 0:LANES]],
                                axis=0).astype(BF16) for it in items}

    for ch in range(chunks):
        rows = slice(ch * c, (ch + 1) * c)
        z0 = [s_ref[j] for j in range(pairs)]
        inter = [_dot(lhs5[(ch, j)], z0[j].astype(BF16)) for j in range(pairs)]
        for j in range(pairs):
            it = (ch, j)
            ps = slice(j * LANES, (j + 1) * LANES)
            u = inter[j][0:c] + x[it][:, LANES:2 * LANES]
            y = inter[j][c:2 * c] + top[it][:, LANES:2 * LANES]
            uv = jnp.concatenate([u.astype(BF16), v_p[it]], axis=0)
            s_ref[j] = z0[j] * w_t[it] + jnp.where(block_diag, _dot(bk_t[it], uv), 0.0)
            yc = y - head_sums(y) * (1.0 / n)
            yn = yc * lax.rsqrt(head_sums(yc * yc) * (1.0 / n) + GN_EPS)
            out = (yn * lnw_ref[:, ps] + lnb_ref[:, ps] + bonus[ch][:, ps]) * g_ref[rows, ps]
            o_ref[rows, ps] = out.astype(o_ref.dtype)


def rwkv_scan(rkv, ld, a, g, k_k, k_a, r_k, ln_w, ln_b, *, batch, seq, heads, chunks):
    t, d = ld.shape
    rows = chunks * RWKV_CHUNK
    nblk = seq // rows
    width = heads * HEAD_DIM

    def data(b, hg, c):
        return (b * nblk + c, hg)

    def stacked(m):
        return pl.BlockSpec((None, rows, width), lambda b, hg, c: (m, b * nblk + c, hg))

    def param():
        return pl.BlockSpec((1, width), lambda b, hg, c: (0, hg))

    return pl.pallas_call(
        functools.partial(_rwkv_scan_kernel, heads=heads, chunks=chunks),
        out_shape=jax.ShapeDtypeStruct((t, d), BF16),
        grid=(batch, d // width, nblk),
        in_specs=[stacked(0), stacked(1), stacked(2),
                  pl.BlockSpec((rows, width), data), pl.BlockSpec((rows, width), data),
                  pl.BlockSpec((rows, width), data),
                  param(), param(), param(), param(), param()],
        out_specs=pl.BlockSpec((rows, width), data),
        scratch_shapes=[pltpu.VMEM((heads // 2, LANES, LANES), F32)],
        compiler_params=_cparams("parallel", "parallel", "arbitrary"),
        name="rwkv_scan",
    )(rkv, rkv, rkv, ld, a, g, k_k[None, :], k_a[None, :], r_k.reshape(1, d), ln_w[None, :], ln_b[None, :])


def _attn_ssd_layer(x, batch, seq, norm_g, w_in, conv_w, conv_b, dt_bias, a_log, d_skip, ssd_norm,
                    q_norm, k_norm, sinks, w_out, cos_t, sin_t):
    q_end, k_end, v_end, z_end, xbc_end = 1024, 1280, 1536, 2560, 4096
    x_end = z_end + D_INNER
    w_in_p = jnp.concatenate(
        [w_in[:, :q_end], w_in[:, v_end:z_end], w_in[:, z_end:x_end], w_in[:, q_end:k_end], w_in[:, k_end:v_end],
         w_in[:, x_end:xbc_end], w_in[:, xbc_end:], jnp.zeros((D_MODEL, LANES - SSM_HEADS), w_in.dtype)],
        axis=1).astype(BF16)
    proj = norm_matmul(x, norm_g[None, :], w_in_p, tm=_rows(x, TM_PROJ), tn=PROJ_W // 3)
    y_attn = swa_attention(proj, cos_t, sin_t, jnp.tile(q_norm, 2)[None, :], jnp.tile(k_norm, 2)[None, :], sinks,
                           batch=batch, seq=seq)
    y_ssd = ssd_mixer(proj, conv_w, conv_b, dt_bias, a_log, d_skip, ssd_norm, batch=batch, seq=seq)
    w_out_b = w_out.astype(BF16)
    return matmul_residual(x, [y_attn, y_ssd], [w_out_b[:Q_W], w_out_b[Q_W:]], tm=_rows(x, TM_OUT), tn=D_MODEL)


def _rwkv_layer(x, batch, seq, norm_g, mix, w0, w1, w2, a0, a1, a2, g1, g2, k_k, k_a, r_k,
                w_r, w_k, w_v, w_o, ln_w, ln_b):
    mix_p = jnp.stack([mix[0], mix[2], mix[3], mix[1], mix[4], mix[5]])
    xs = rwkv_mix(x, norm_g[None, :], mix_p, tm=256, seq=seq)
    rkv = batched_matmul(xs, jnp.stack([w_r, w_k, w_v]).astype(BF16), tm=_rows(x, TM_BMM), tn=D_MODEL)
    lora_pad = LANES - w1.shape[1]
    w1_p = jnp.pad(w1, ((0, 0), (0, lora_pad))).astype(BF16)
    a1_p = jnp.pad(a1, ((0, 0), (0, lora_pad))).astype(BF16)
    w2_p = jnp.pad(w2, ((0, lora_pad), (0, 0))).astype(BF16)
    a2_p = jnp.pad(a2, ((0, lora_pad), (0, 0))).astype(BF16)
    ld, a, g = rwkv_lora(xs, w1_p, a1_p, g1.astype(BF16), w2_p, a2_p, g2.astype(BF16),
                         w0[None, :], a0[None, :], tm=256)
    yg = rwkv_scan(rkv, ld, a, g, k_k, k_a, r_k, ln_w, ln_b, batch=batch, seq=seq, heads=8, chunks=4)
    return matmul_residual(x, [yg], [w_o.astype(BF16)], tm=_rows(x, TM_OUT), tn=D_MODEL)


def kernel(x, norm_mix, norm_ffn, w_up, w_down, ev_w_in, ev_conv_w, ev_conv_b, ev_dt_bias, ev_a_log, ev_d_skip, ev_ssd_norm, ev_q_norm, ev_k_norm, ev_sinks, ev_w_out, od_mix, od_w0, od_w1, od_w2, od_a0, od_a1, od_a2, od_g1, od_g2, od_k_k, od_k_a, od_r_k, od_w_r, od_w_k, od_w_v, od_w_o, od_ln_w, od_ln_b):
    batch, seq, d = x.shape
    depth = norm_mix.shape[0]
    inv_freq = ROPE_THETA ** (-jnp.arange(0, HEAD_DIM, 2, dtype=F32) / HEAD_DIM)
    ang = jnp.arange(seq, dtype=F32)[:, None] * inv_freq[None, :]
    cos, sin = jnp.cos(ang), jnp.sin(ang)
    cos_t = jnp.concatenate([cos, cos, cos, cos], axis=1)
    sin_t = jnp.concatenate([-sin, sin, -sin, sin], axis=1)
    xf = x.reshape(batch * seq, d)
    for i in range(depth):
        j = i // 2
        if i % 2 == 0:
            xf = _attn_ssd_layer(xf, batch, seq, norm_mix[i], ev_w_in[j], ev_conv_w[j], ev_conv_b[j],
                                 ev_dt_bias[j], ev_a_log[j], ev_d_skip[j], ev_ssd_norm[j], ev_q_norm[j],
                                 ev_k_norm[j], ev_sinks[j], ev_w_out[j], cos_t, sin_t)
        else:
            xf = _rwkv_layer(xf, batch, seq, norm_mix[i], od_mix[j], od_w0[j], od_w1[j], od_w2[j], od_a0[j],
                             od_a1[j], od_a2[j], od_g1[j], od_g2[j], od_k_k[j], od_k_a[j], od_r_k[j],
                             od_w_r[j], od_w_k[j], od_w_v[j], od_w_o[j], od_ln_w[j], od_ln_b[j])
        xf = mlp_block(xf, norm_ffn[i][None, :], w_up[i].astype(BF16), w_down[i].astype(BF16),
                       tm=_rows(xf, TM_MLP), tf=TF_MLP)
    return xf.reshape(batch, seq, d)
```

```python
import functools

import jax
import jax.numpy as jnp
from jax import lax
from jax.experimental import pallas as pl
from jax.experimental.pallas import tpu as pltpu

F32 = jnp.float32
BF16 = jnp.bfloat16

D_MODEL = 2048
HEAD_DIM = 64
N_Q_HEADS = 16
N_KV_HEADS = 4
Q_REP = N_Q_HEADS // N_KV_HEADS
WINDOW = 128
ROPE_THETA = 10000.0
SSM_HEADS = 16
SSM_HEAD_DIM = 64
D_INNER = 1024
D_STATE = 128
N_GROUPS = 2
CONV_WIDTH = 4
CHUNK = 128
RWKV_HEADS = 32
D_FF = 4 * D_MODEL
NORM_EPS = 1e-6
GN_EPS = 64e-5
Q_W = 1024
KV_W = 256
BC_W = 256

LANES = 128
SUBLANES = 8
VMEM_LIMIT_BYTES = 56 * 1024 * 1024

PROJ_W = 3 * 1024 + 4 * 256 + LANES
RWKV_CHUNK = 64


TM_PROJ = 1024
TM_OUT = 512
TM_BMM = 1024
TM_MLP = 1024
TF_MLP = 512


def _rows(x, want):
    return min(want, x.shape[0])


def _cparams(*sem):
    return pltpu.CompilerParams(dimension_semantics=sem, vmem_limit_bytes=VMEM_LIMIT_BYTES)


def _rms(x, g):
    inv = lax.rsqrt(jnp.mean(x * x, axis=-1, keepdims=True) + NORM_EPS)
    return x * inv * g


def _silu(x):
    return x * (1.0 / (1.0 + jnp.exp(-x)))


def _sigmoid(x):
    return 1.0 / (1.0 + jnp.exp(-x))


def _softplus(x):
    return jnp.maximum(x, 0.0) + jnp.log1p(jnp.exp(-jnp.abs(x)))


def _dot(a, b):
    return jnp.dot(a, b, preferred_element_type=F32)


def _dot_nt(a, b):
    return lax.dot_general(a, b, (((1,), (1,)), ((), ())), preferred_element_type=F32)


def _dot_tn(a, b):
    return lax.dot_general(a, b, (((0,), (0,)), ((), ())), preferred_element_type=F32)


def _split3(x):
    hi = x.astype(BF16)
    r1 = x - hi.astype(F32)
    mid = r1.astype(BF16)
    lo = (r1 - mid.astype(F32)).astype(BF16)
    return hi, mid, lo


def _dot_exact_rhs(sel_b, x):
    hi, mid, lo = _split3(x)
    return _dot(sel_b, hi) + _dot(sel_b, mid) + _dot(sel_b, lo)


def _dot_exact_lhs(x, sel_b):
    hi, mid, lo = _split3(x)
    return _dot(hi, sel_b) + _dot(mid, sel_b) + _dot(lo, sel_b)


def _norm_matmul_kernel(x_ref, g_ref, w_ref, o_ref, h_ref):
    @pl.when(pl.program_id(1) == 0)
    def _():
        h_ref[...] = _rms(x_ref[...], g_ref[...]).astype(BF16)

    o_ref[...] = _dot(h_ref[...], w_ref[...])


def norm_matmul(x, g, w, *, tm, tn):
    t, d = x.shape
    n = w.shape[1]
    return pl.pallas_call(
        _norm_matmul_kernel,
        out_shape=jax.ShapeDtypeStruct((t, n), F32),
        grid=(t // tm, n // tn),
        in_specs=[pl.BlockSpec((tm, d), lambda i, j: (i, 0)),
                  pl.BlockSpec((1, d), lambda i, j: (0, 0)),
                  pl.BlockSpec((d, tn), lambda i, j: (0, j))],
        out_specs=pl.BlockSpec((tm, tn), lambda i, j: (i, j)),
        scratch_shapes=[pltpu.VMEM((tm, d), BF16)],
        compiler_params=_cparams("parallel", "arbitrary"),
        name="norm_matmul",
    )(x, g, w)


def _matmul_residual_kernel(*refs, n_in):
    res_ref = refs[0]
    a_refs = refs[1:1 + n_in]
    w_refs = refs[1 + n_in:1 + 2 * n_in]
    o_ref = refs[1 + 2 * n_in]
    acc = res_ref[...]
    for a_ref, w_ref in zip(a_refs, w_refs):
        acc = acc + _dot(a_ref[...], w_ref[...])
    o_ref[...] = acc


def matmul_residual(res, a_list, w_list, *, tm, tn):
    t, n = res.shape
    n_in = len(a_list)
    in_specs = [pl.BlockSpec((tm, tn), lambda i, j: (i, j))]
    in_specs += [pl.BlockSpec((tm, a.shape[1]), lambda i, j: (i, 0)) for a in a_list]
    in_specs += [pl.BlockSpec((w.shape[0], tn), lambda i, j: (0, j)) for w in w_list]
    return pl.pallas_call(
        functools.partial(_matmul_residual_kernel, n_in=n_in),
        out_shape=jax.ShapeDtypeStruct((t, n), F32),
        grid=(t // tm, n // tn),
        in_specs=in_specs,
        out_specs=pl.BlockSpec((tm, tn), lambda i, j: (i, j)),
        compiler_params=_cparams("parallel", "arbitrary"),
        name="matmul_residual",
    )(res, *a_list, *w_list)


def _mlp_kernel(x_ref, g_ref, wu_ref, wd_ref, o_ref, h_ref):
    @pl.when(pl.program_id(1) == 0)
    def _():
        x = x_ref[...]
        h_ref[...] = _rms(x, g_ref[...]).astype(BF16)
        o_ref[...] = x

    u = _dot(h_ref[...], wu_ref[...])
    a = jnp.square(jnp.maximum(u, 0.0)).astype(BF16)
    o_ref[...] += _dot(a, wd_ref[...])


def mlp_block(x, g, w_up, w_down, *, tm, tf):
    t, d = x.shape
    f = w_up.shape[1]
    return pl.pallas_call(
        _mlp_kernel,
        out_shape=jax.ShapeDtypeStruct((t, d), F32),
        grid=(t // tm, f // tf),
        in_specs=[pl.BlockSpec((tm, d), lambda i, j: (i, 0)),
                  pl.BlockSpec((1, d), lambda i, j: (0, 0)),
                  pl.BlockSpec((d, tf), lambda i, j: (0, j)),
                  pl.BlockSpec((tf, d), lambda i, j: (j, 0))],
        out_specs=pl.BlockSpec((tm, d), lambda i, j: (i, 0)),
        scratch_shapes=[pltpu.VMEM((tm, d), BF16)],
        compiler_params=_cparams("parallel", "arbitrary"),
        name="mlp_block",
    )(x, g, w_up, w_down)


def _low_lanes(shape):
    return lax.broadcasted_iota(jnp.int32, shape, 1) % LANES < HEAD_DIM


def _block_rows(arr):
    low = _low_lanes(arr.shape)
    zero = jnp.zeros_like(arr)
    return jnp.concatenate([jnp.where(low, arr, zero), jnp.where(low, zero, arr)], axis=0)


def _head_sums(blk):
    low = _low_lanes(blk.shape)
    s_lo = jnp.sum(jnp.where(low, blk, 0.0), axis=-1, keepdims=True)
    s_hi = jnp.sum(jnp.where(low, 0.0, blk), axis=-1, keepdims=True)
    return jnp.where(low, s_lo, s_hi)


def _both_halves(pair):
    swapped = pltpu.roll(pair, HEAD_DIM, axis=1)
    low = _low_lanes(pair.shape)
    return jnp.where(low, pair, swapped), jnp.where(low, swapped, pair)


def _norm_rope(xp, gain, cos, sin):
    half = HEAD_DIM // 2
    ms = _head_sums(xp * xp) * (1.0 / HEAD_DIM)
    xn = xp * lax.rsqrt(ms + NORM_EPS) * gain
    from_below = pltpu.roll(xn, half, axis=1)
    from_above = pltpu.roll(xn, LANES - half, axis=1)
    first_half = lax.broadcasted_iota(jnp.int32, xp.shape, 1) % HEAD_DIM < half
    return xn * cos + jnp.where(first_half, from_above, from_below) * sin


def _attn_kernel(sinks_ref, q_ref, kc_ref, kp_ref, vc_ref, vp_ref, cc_ref, sc_ref, cp_ref, sp_ref,
                 qn_ref, kn_ref, o_ref):
    n = pl.program_id(1)
    w = WINDOW
    cos_c, sin_c = cc_ref[...], sc_ref[...]
    cos_pc = jnp.concatenate([cp_ref[...], cos_c], axis=0)
    sin_pc = jnp.concatenate([sp_ref[...], sin_c], axis=0)
    qn, kn = qn_ref[...], kn_ref[...]

    rows = lax.broadcasted_iota(jnp.int32, (Q_REP * w, 2 * w), 0)
    qi = rows % w
    kj = lax.broadcasted_iota(jnp.int32, (Q_REP * w, 2 * w), 1)
    lo = jnp.where(n == 0, w - 1, 0)
    mask = (kj > qi) & (kj <= qi + w) & (kj > lo)

    k_dup, v_dup = [], []
    for j in range(N_KV_HEADS // 2):
        ps = slice(j * LANES, (j + 1) * LANES)
        k_pair = _norm_rope(jnp.concatenate([kp_ref[:, ps], kc_ref[:, ps]], axis=0), kn, cos_pc, sin_pc)
        k_dup += [t.astype(BF16) for t in _both_halves(k_pair)]
        v_dup += [t.astype(BF16) for t in _both_halves(jnp.concatenate([vp_ref[:, ps], vc_ref[:, ps]], axis=0))]
    q_rows = []
    for i in range(N_Q_HEADS // 2):
        qp = _norm_rope(q_ref[:, i * LANES:(i + 1) * LANES], qn, cos_c, sin_c) * (HEAD_DIM ** -0.5)
        q_rows.append(_block_rows(qp.astype(BF16)))
    s = [_dot_nt(jnp.concatenate(q_rows[2 * g:2 * g + 2], axis=0), k_dup[g]) for g in range(N_KV_HEADS)]
    p, denom = [], []
    for g in range(N_KV_HEADS):
        sink = jnp.concatenate([jnp.full((w, 1), sinks_ref[g * Q_REP + r], F32) for r in range(Q_REP)], axis=0)
        sg = jnp.where(mask, s[g], -jnp.inf)
        m = jnp.maximum(jnp.max(sg, axis=-1, keepdims=True), sink)
        e = jnp.exp(sg - m)
        denom.append(jnp.sum(e, axis=-1, keepdims=True) + jnp.exp(sink - m))
        p.append(e.astype(BF16))
    o = [_dot(p[g], v_dup[g]) / denom[g] for g in range(N_KV_HEADS)]
    low = _low_lanes((w, LANES))
    for i in range(N_Q_HEADS // 2):
        og = o[i // 2]
        r0 = (i % 2) * 2 * w
        o_ref[:, i * LANES:(i + 1) * LANES] = jnp.where(low, og[r0:r0 + w], og[r0 + w:r0 + 2 * w]).astype(o_ref.dtype)


def swa_attention(proj, cos_t, sin_t, q_norm, k_norm, sinks, *, batch, seq):
    nb = seq // WINDOW
    kcol = 3 * 1024 // KV_W
    vcol = kcol + 1

    def cur(b, n):
        return b * nb + n

    def prev(b, n):
        return b * nb + jnp.maximum(n - 1, 0)

    return pl.pallas_call(
        _attn_kernel,
        out_shape=jax.ShapeDtypeStruct((batch * seq, Q_W), BF16),
        grid=(batch, nb),
        in_specs=[pl.BlockSpec(memory_space=pltpu.SMEM),
                  pl.BlockSpec((WINDOW, Q_W), lambda b, n: (cur(b, n), 0)),
                  pl.BlockSpec((WINDOW, KV_W), lambda b, n: (cur(b, n), kcol)),
                  pl.BlockSpec((WINDOW, KV_W), lambda b, n: (prev(b, n), kcol)),
                  pl.BlockSpec((WINDOW, KV_W), lambda b, n: (cur(b, n), vcol)),
                  pl.BlockSpec((WINDOW, KV_W), lambda b, n: (prev(b, n), vcol)),
                  pl.BlockSpec((WINDOW, LANES), lambda b, n: (n, 0)),
                  pl.BlockSpec((WINDOW, LANES), lambda b, n: (n, 0)),
                  pl.BlockSpec((WINDOW, LANES), lambda b, n: (jnp.maximum(n - 1, 0), 0)),
                  pl.BlockSpec((WINDOW, LANES), lambda b, n: (jnp.maximum(n - 1, 0), 0)),
                  pl.BlockSpec((1, LANES), lambda b, n: (0, 0)),
                  pl.BlockSpec((1, LANES), lambda b, n: (0, 0))],
        out_specs=pl.BlockSpec((WINDOW, Q_W), lambda b, n: (cur(b, n), 0)),
        compiler_params=_cparams("parallel", "parallel"),
        name="swa_attention",
    )(sinks, proj, proj, proj, proj, proj, cos_t, sin_t, cos_t, sin_t, q_norm, k_norm)


def _ssd_kernel(z_ref, x_ref, b_ref, c_ref, dt_ref, cwx_ref, cwb_ref, cwc_ref, cbx_ref, cbb_ref, cbc_ref,
                dtb_ref, alog_ref, dsk_ref, nrm_ref, exp_ref, o_ref,
                xpx_ref, xpb_ref, xpc_ref, state_ref, y_ref):
    q = CHUNK
    hist = SUBLANES

    @pl.when(pl.program_id(1) == 0)
    def _():
        xpx_ref[0:hist, :] = jnp.zeros((hist, D_INNER), F32)
        xpb_ref[0:hist, :] = jnp.zeros((hist, BC_W), F32)
        xpc_ref[0:hist, :] = jnp.zeros((hist, BC_W), F32)
        state_ref[...] = jnp.zeros_like(state_ref)

    def conv(raw_ref, pad_ref, cw_ref, cb_ref):
        pad_ref[hist:hist + q, :] = raw_ref[...]
        acc = cb_ref[...]
        for j in range(CONV_WIDTH):
            start = hist - (CONV_WIDTH - 1) + j
            acc = acc + pad_ref[start:start + q, :] * cw_ref[j:j + 1, :]
        pad_ref[0:hist, :] = pad_ref[q:q + hist, :]
        return _silu(acc)

    xs = conv(x_ref, xpx_ref, cwx_ref, cbx_ref)
    bm = conv(b_ref, xpb_ref, cwb_ref, cbb_ref)
    cm = conv(c_ref, xpc_ref, cwc_ref, cbc_ref)

    dt = _softplus(dt_ref[...] + dtb_ref[...])
    dta = dt * (-jnp.exp(alog_ref[...]))
    ri = lax.broadcasted_iota(jnp.int32, (q, q), 0)
    ci = lax.broadcasted_iota(jnp.int32, (q, q), 1)
    causal = ri >= ci
    tril = jnp.where(causal, 1.0, 0.0).astype(BF16)
    cs = _dot_exact_rhs(tril, dta)
    cs_t = cs.T
    cs_last = cs[q - 1:q, :]
    expand = exp_ref[...]
    stacked = jnp.concatenate([dt, jnp.exp(cs), jnp.exp(cs_last - cs)], axis=0).astype(BF16)
    wide = _dot(stacked, expand)
    dt_w, ecs_w, dout_w = wide[0:q], wide[q:2 * q], wide[2 * q:3 * q]
    cdec_w = _dot_exact_lhs(jnp.broadcast_to(jnp.exp(cs_last), (SUBLANES, LANES)), expand)[0:1]

    xdt = xs * dt_w
    xdt_b = xdt.astype(BF16)
    xdec_b = (xdt * dout_w).astype(BF16)
    hg = SSM_HEADS // N_GROUPS
    gw = hg * SSM_HEAD_DIM
    for g in range(N_GROUPS):
        cg = cm[:, g * D_STATE:(g + 1) * D_STATE].astype(BF16)
        bg = bm[:, g * D_STATE:(g + 1) * D_STATE].astype(BF16)
        cb = _dot_nt(cg, bg)
        gs = slice(g * gw, (g + 1) * gw)
        prev = state_ref[:, gs]
        y_ref[:, gs] = _dot(cg, prev.astype(BF16)) * ecs_w[:, gs]
        for hh in range(hg):
            h = g * hg + hh
            seg = cs[:, h:h + 1] - cs_t[h:h + 1, :]
            lm = jnp.exp(jnp.where(causal, seg, -jnp.inf))
            hs = slice(h * SSM_HEAD_DIM, (h + 1) * SSM_HEAD_DIM)
            y_ref[:, hs] += _dot((cb * lm).astype(BF16), xdt_b[:, hs])
        state_ref[:, gs] = prev * cdec_w[:, gs] + _dot_tn(bg, xdec_b[:, gs])

    y = y_ref[...] + dsk_ref[...] * xs
    y = y * _silu(z_ref[...])
    for g in range(N_GROUPS):
        gs = slice(g * gw, (g + 1) * gw)
        yg = y[:, gs]
        inv = lax.rsqrt(jnp.mean(yg * yg, axis=-1, keepdims=True) + NORM_EPS)
        o_ref[:, gs] = (yg * inv * nrm_ref[:, gs]).astype(o_ref.dtype)


def ssd_mixer(proj, conv_w, conv_b, dt_bias, a_log, d_skip, ssd_norm, *, batch, seq):
    nc = seq // CHUNK
    pad = LANES - SSM_HEADS
    cwx, cwb, cwc = conv_w[:, :D_INNER], conv_w[:, D_INNER:D_INNER + BC_W], conv_w[:, D_INNER + BC_W:]
    cbx, cbb, cbc = (conv_b[None, :D_INNER], conv_b[None, D_INNER:D_INNER + BC_W], conv_b[None, D_INNER + BC_W:])
    dtb = jnp.pad(dt_bias, (0, pad))[None, :]
    alog = jnp.pad(a_log, (0, pad))[None, :]
    dsk = jnp.repeat(d_skip, SSM_HEAD_DIM)[None, :]
    head_of_channel = jnp.arange(D_INNER) // SSM_HEAD_DIM
    expand = (jnp.arange(LANES)[:, None] == head_of_channel[None, :]).astype(BF16)

    def row(b, c):
        return b * nc + c

    def full(shape):
        return pl.BlockSpec(shape, lambda b, c: (0, 0))

    hist = SUBLANES
    return pl.pallas_call(
        _ssd_kernel,
        out_shape=jax.ShapeDtypeStruct((batch * seq, D_INNER), BF16),
        grid=(batch, nc),
        in_specs=[pl.BlockSpec((CHUNK, D_INNER), lambda b, c: (row(b, c), 1)),
                  pl.BlockSpec((CHUNK, D_INNER), lambda b, c: (row(b, c), 2)),
                  pl.BlockSpec((CHUNK, BC_W), lambda b, c: (row(b, c), 14)),
                  pl.BlockSpec((CHUNK, BC_W), lambda b, c: (row(b, c), 15)),
                  pl.BlockSpec((CHUNK, LANES), lambda b, c: (row(b, c), 32)),
                  full((CONV_WIDTH, D_INNER)), full((CONV_WIDTH, BC_W)), full((CONV_WIDTH, BC_W)),
                  full((1, D_INNER)), full((1, BC_W)), full((1, BC_W)),
                  full((1, LANES)), full((1, LANES)), full((1, D_INNER)), full((1, D_INNER)),
                  full((LANES, D_INNER))],
        out_specs=pl.BlockSpec((CHUNK, D_INNER), lambda b, c: (row(b, c), 0)),
        scratch_shapes=[pltpu.VMEM((CHUNK + hist, D_INNER), F32),
                        pltpu.VMEM((CHUNK + hist, BC_W), F32),
                        pltpu.VMEM((CHUNK + hist, BC_W), F32),
                        pltpu.VMEM((D_STATE, D_INNER), F32),
                        pltpu.VMEM((CHUNK, D_INNER), F32)],
        compiler_params=_cparams("parallel", "arbitrary"),
        name="ssd_mixer",
    )(proj, proj, proj, proj, proj, cwx, cwb, cwc, cbx, cbb, cbc, dtb, alog, dsk, ssd_norm[None, :], expand)


def _rwkv_mix_kernel(x_ref, xp_ref, g_ref, mix_ref, o_ref, hp_ref, *, tm, seq):
    i = pl.program_id(0)
    hist = SUBLANES
    g = g_ref[...]
    h = _rms(x_ref[...], g)
    keep = jnp.where((i * tm) % seq == 0, 0.0, 1.0)
    hp_ref[0:hist, :] = _rms(xp_ref[...], g) * keep
    hp_ref[hist:hist + tm, :] = h
    xx = hp_ref[hist - 1:hist - 1 + tm, :] - h
    for m in range(6):
        o_ref[m] = (h + xx * mix_ref[m:m + 1, :]).astype(o_ref.dtype)


def rwkv_mix(x, g, mix, *, tm, seq):
    t, d = x.shape
    hist = SUBLANES
    return pl.pallas_call(
        functools.partial(_rwkv_mix_kernel, tm=tm, seq=seq),
        out_shape=jax.ShapeDtypeStruct((6, t, d), BF16),
        grid=(t // tm,),
        in_specs=[pl.BlockSpec((tm, d), lambda i: (i, 0)),
                  pl.BlockSpec((hist, d), lambda i: (jnp.maximum(i * (tm // hist) - 1, 0), 0)),
                  pl.BlockSpec((1, d), lambda i: (0, 0)),
                  pl.BlockSpec((6, d), lambda i: (0, 0))],
        out_specs=pl.BlockSpec((6, tm, d), lambda i: (0, i, 0)),
        scratch_shapes=[pltpu.VMEM((tm + hist, d), F32)],
        compiler_params=_cparams("parallel"),
        name="rwkv_mix",
    )(x, x, g, mix)


def _bmm_kernel(x_ref, w_ref, o_ref):
    o_ref[...] = _dot(x_ref[...], w_ref[...]).astype(o_ref.dtype)


def batched_matmul(xs, ws, *, tm, tn):
    nm, d, n = ws.shape
    t = xs.shape[1]
    return pl.pallas_call(
        _bmm_kernel,
        out_shape=jax.ShapeDtypeStruct((nm, t, n), BF16),
        grid=(nm, t // tm, n // tn),
        in_specs=[pl.BlockSpec((None, tm, d), lambda m, i, j: (m, i, 0)),
                  pl.BlockSpec((None, d, tn), lambda m, i, j: (m, 0, j))],
        out_specs=pl.BlockSpec((None, tm, tn), lambda m, i, j: (m, i, j)),
        compiler_params=_cparams("parallel", "parallel", "arbitrary"),
        name="batched_matmul",
    )(xs, ws)


def _lora_kernel(xw_ref, xa_ref, xg_ref, w1_ref, a1_ref, g1_ref, w2_ref, a2_ref, g2_ref, w0_ref, a0_ref,
                 ld_ref, a_ref, g_ref):
    tw = jnp.tanh(_dot(xw_ref[...], w1_ref[...])).astype(BF16)
    zw = w0_ref[...] + _dot(tw, w2_ref[...])
    logw = -_softplus(-zw) - 0.5
    ld_ref[...] = -jnp.exp(logw)
    ta = _dot(xa_ref[...], a1_ref[...]).astype(BF16)
    a_ref[...] = _sigmoid(a0_ref[...] + _dot(ta, a2_ref[...])).astype(a_ref.dtype)
    tg = _sigmoid(_dot(xg_ref[...], g1_ref[...])).astype(BF16)
    g_ref[...] = _dot(tg, g2_ref[...]).astype(g_ref.dtype)


def rwkv_lora(xs, w1, a1, g1, w2, a2, g2, w0, a0, *, tm):
    _, t, d = xs.shape

    def full(a):
        return pl.BlockSpec(a.shape, lambda i: (0, 0))

    return pl.pallas_call(
        _lora_kernel,
        out_shape=(jax.ShapeDtypeStruct((t, d), F32), jax.ShapeDtypeStruct((t, d), BF16),
                   jax.ShapeDtypeStruct((t, d), BF16)),
        grid=(t // tm,),
        in_specs=[pl.BlockSpec((None, tm, d), lambda i: (3, i, 0)),
                  pl.BlockSpec((None, tm, d), lambda i: (4, i, 0)),
                  pl.BlockSpec((None, tm, d), lambda i: (5, i, 0)),
                  full(w1), full(a1), full(g1), full(w2), full(a2), full(g2), full(w0), full(a0)],
        out_specs=(pl.BlockSpec((tm, d), lambda i: (i, 0)),) * 3,
        compiler_params=_cparams("parallel"),
        name="rwkv_lora",
    )(xs, xs, xs, w1, a1, g1, w2, a2, g2, w0, a0)


def _rwkv_scan_kernel(r_ref, k_ref, v_ref, ld_ref, a_ref, g_ref, kk_ref, ka_ref, rk_ref, lnw_ref, lnb_ref,
                      o_ref, s_ref, *, heads, chunks):
    c = RWKV_CHUNK
    n = HEAD_DIM

    @pl.when(pl.program_id(2) == 0)
    def _():
        s_ref[...] = jnp.zeros_like(s_ref)

    ri = lax.broadcasted_iota(jnp.int32, (c, c), 0)
    ci = lax.broadcasted_iota(jnp.int32, (c, c), 1)
    tril = jnp.where(ri >= ci, 1.0, 0.0).astype(BF16)
    ri2 =lax.broadcasted_iota(jnp.int32, (2 * c, 4 * c), 0)
    ci2 = lax.broadcasted_iota(jnp.int32, (2 * c, 4 * c), 1) % c
    gmask = ci2 <= jnp.where(ri2 < c, ri2 - 1, ri2 - c)
    zeros_sq = jnp.zeros((LANES, LANES), BF16)
    n_steps = c.bit_length() - 1
    pairs = heads // 2

    block_rows, head_sums = _block_rows, _head_sums

    def all_head_sums(arr):
        return jnp.concatenate([head_sums(arr[:, j * LANES:(j + 1) * LANES]) for j in range(pairs)], axis=1)

    block_diag = (lax.broadcasted_iota(jnp.int32, (LANES, LANES), 0) < n) == _low_lanes((LANES, LANES))

    items = [(ch, j) for ch in range(chunks) for j in range(pairs)]
    at_p, rt_p, v_p, bd_v, bk_t, w_t, bonus, lhs1, rhs1 = ({} for _ in range(9))
    for ch in range(chunks):
        rows = slice(ch * c, (ch + 1) * c)
        ld = ld_ref[rows, :]
        cl = _dot_exact_rhs(tril, ld)
        cl_last = cl[c - 1:c, :]
        e_inc = jnp.exp(cl)
        e_neg = jnp.exp(-cl)
        e_exc = jnp.exp(cl - ld)
        e_bar = jnp.exp(cl_last - cl)
        w_all = jnp.exp(cl_last)
        r_all, k_all, v_all, a_all = (t_ref[rows, :].astype(F32) for t_ref in (r_ref, k_ref, v_ref, a_ref))
        kk_all = k_all * kk_ref[...]
        kk_all = kk_all * lax.rsqrt(jnp.maximum(all_head_sums(kk_all * kk_all), 1e-24))
        kf_all = k_all * (1.0 + (a_all - 1.0) * ka_ref[...])
        bb_all = kk_all * a_all
        bonus[ch] = all_head_sums(r_all * kf_all * rk_ref[...]) * v_all
        at_all = (-kk_all * e_exc).astype(BF16)
        rt_all = (r_all * e_inc).astype(BF16)
        bt_all = (bb_all * e_neg).astype(BF16)
        kt_all = (kf_all * e_neg).astype(BF16)
        vb_all = v_all.astype(BF16)
        bbar_all = bb_all * e_bar
        kbar_all = kf_all * e_bar
        w_rows = jnp.broadcast_to(w_all, (2 * c, heads * n))
        for j in range(pairs):
            it = (ch, j)
            ps = slice(j * LANES, (j + 1) * LANES)
            at_p[it], rt_p[it], v_p[it] = at_all[:, ps], rt_all[:, ps], vb_all[:, ps]
            lhs1[it] = jnp.concatenate([at_p[it], rt_p[it]], axis=0)
            rhs1[it] = jnp.concatenate([block_rows(bt_all[:, ps]), block_rows(kt_all[:, ps])], axis=0)
            bd_v[it] = block_rows(v_p[it])
            bk_t[it] = jnp.concatenate([bbar_all[:, ps], kbar_all[:, ps]], axis=0).T.astype(BF16)
            w_t[it] = w_rows[:, ps].T

    gm = {it: jnp.where(gmask, _dot_nt(lhs1[it], rhs1[it]), 0.0) for it in items}
    a_r = {it: gm[it][c:2 * c, :].astype(BF16) for it in items}
    p_b = {it: gm[it][0:c, 0:LANES].astype(BF16) for it in items}
    akv = {it: _dot(gm[it][0:c, LANES:2 * LANES].astype(BF16), bd_v[it]) for it in items}
    x = {it: jnp.concatenate([at_p[it].astype(F32), akv[it]], axis=1) for it in items}
    for step in range(n_steps):
        x = {it: x[it] + _dot(p_b[it], block_rows(x[it].astype(BF16))) for it in items}
        if step + 1 < n_steps:
            p_b = {it: _dot(p_b[it], block_rows(p_b[it])).astype(BF16) for it in items}
    top = {}
    for it in items:
        rhs = jnp.concatenate([block_rows(x[it].astype(BF16)),
                               jnp.concatenate([zeros_sq, bd_v[it]], axis=1)], axis=0)
        top[it] = _dot(a_r[it], rhs)
    lhs5 = {it: jnp.concatenate([x[it][:, 0:LANES], rt_p[it].astype(F32) + top[it][:, 0:LANES]],
                                axis=0).astype(BF16) for it in items}

    for ch in range(chunks):
        rows = slice(ch * c, (ch + 1) * c)
        z0 = [s_ref[j] for j in range(pairs)]
        inter = [_dot(lhs5[(ch, j)], z0[j].astype(BF16)) for j in range(pairs)]
        for j in range(pairs):
            it = (ch, j)
            ps = slice(j * LANES, (j + 1) * LANES)
            u = inter[j][0:c] + x[it][:, LANES:2 * LANES]
            y = inter[j][c:2 * c] + top[it][:, LANES:2 * LANES]
            uv = jnp.concatenate([u.astype(BF16), v_p[it]], axis=0)
            s_ref[j] = z0[j] * w_t[it] + jnp.where(block_diag, _dot(bk_t[it], uv), 0.0)
            yc = y - head_sums(y) * (1.0 / n)
            yn = yc * lax.rsqrt(head_sums(yc * yc) * (1.0 / n) + GN_EPS)
            out = (yn * lnw_ref[:, ps] + lnb_ref[:, ps] + bonus[ch][:, ps]) * g_ref[rows, ps].astype(F32)
            o_ref[rows, ps] = out.astype(o_ref.dtype)


def rwkv_scan(rkv, ld, a, g, k_k, k_a, r_k, ln_w, ln_b, *, batch, seq, heads, chunks):
    t, d = ld.shape
    rows = chunks * RWKV_CHUNK
    nblk = seq // rows
    width = heads * HEAD_DIM

    def data(b, hg, c):
        return (b * nblk + c, hg)

    def stacked(m):
        return pl.BlockSpec((None, rows, width), lambda b, hg, c: (m, b * nblk + c, hg))

    def param():
        return pl.BlockSpec((1, width), lambda b, hg, c: (0, hg))

    return pl.pallas_call(
        functools.partial(_rwkv_scan_kernel, heads=heads, chunks=chunks),
        out_shape=jax.ShapeDtypeStruct((t, d), BF16),
        grid=(batch, d // width, nblk),
        in_specs=[stacked(0), stacked(1), stacked(2),
                  pl.BlockSpec((rows, width), data), pl.BlockSpec((rows, width), data),
                  pl.BlockSpec((rows, width), data),
                  param(), param(), param(), param(), param()],
        out_specs=pl.BlockSpec((rows, width), data),
        scratch_shapes=[pltpu.VMEM((heads // 2, LANES, LANES), F32)],
        compiler_params=_cparams("parallel", "parallel", "arbitrary"),
        name="rwkv_scan",
    )(rkv, rkv, rkv, ld, a, g, k_k[None, :], k_a[None, :], r_k.reshape(1, d), ln_w[None, :], ln_b[None, :])


def _attn_ssd_layer(x, batch, seq, norm_g, w_in, conv_w, conv_b, dt_bias, a_log, d_skip, ssd_norm,
                    q_norm, k_norm, sinks, w_out, cos_t, sin_t):
    q_end, k_end, v_end, z_end, xbc_end = 1024, 1280, 1536, 2560, 4096
    x_end = z_end + D_INNER
    w_in_p = jnp.concatenate(
        [w_in[:, :q_end], w_in[:, v_end:z_end], w_in[:, z_end:x_end], w_in[:, q_end:k_end], w_in[:, k_end:v_end],
         w_in[:, x_end:xbc_end], w_in[:, xbc_end:], jnp.zeros((D_MODEL, LANES - SSM_HEADS), w_in.dtype)],
        axis=1).astype(BF16)
    proj = norm_matmul(x, norm_g[None, :], w_in_p, tm=_rows(x, TM_PROJ), tn=PROJ_W // 3)
    y_attn = swa_attention(proj, cos_t, sin_t, jnp.tile(q_norm, 2)[None, :], jnp.tile(k_norm, 2)[None, :], sinks,
                           batch=batch, seq=seq)
    y_ssd = ssd_mixer(proj, conv_w, conv_b, dt_bias, a_log, d_skip, ssd_norm, batch=batch, seq=seq)
    w_out_b = w_out.astype(BF16)
    return matmul_residual(x, [y_attn, y_ssd], [w_out_b[:Q_W], w_out_b[Q_W:]], tm=_rows(x, TM_OUT), tn=D_MODEL)


def _rwkv_layer(x, batch, seq, norm_g, mix, w0, w1, w2, a0, a1, a2, g1, g2, k_k, k_a, r_k,
                w_r, w_k, w_v, w_o, ln_w, ln_b):
    mix_p = jnp.stack([mix[0], mix[2], mix[3], mix[1], mix[4], mix[5]])
    xs = rwkv_mix(x, norm_g[None, :], mix_p, tm=256, seq=seq)
    rkv = batched_matmul(xs, jnp.stack([w_r, w_k, w_v]).astype(BF16), tm=_rows(x, TM_BMM), tn=D_MODEL)
    lora_pad = LANES - w1.shape[1]
    w1_p = jnp.pad(w1, ((0, 0), (0, lora_pad))).astype(BF16)
    a1_p = jnp.pad(a1, ((0, 0), (0, lora_pad))).astype(BF16)
    w2_p = jnp.pad(w2, ((0, lora_pad), (0, 0))).astype(BF16)
    a2_p = jnp.pad(a2, ((0, lora_pad), (0, 0))).astype(BF16)
    ld, a, g = rwkv_lora(xs, w1_p, a1_p, g1.astype(BF16), w2_p, a2_p, g2.astype(BF16),
                         w0[None, :], a0[None, :], tm=256)
    yg = rwkv_scan(rkv, ld, a, g, k_k, k_a, r_k, ln_w, ln_b, batch=batch, seq=seq, heads=16, chunks=4)
    return matmul_residual(x, [yg], [w_o.astype(BF16)], tm=_rows(x, TM_OUT), tn=D_MODEL)


def kernel(x, norm_mix, norm_ffn, w_up, w_down, ev_w_in, ev_conv_w, ev_conv_b, ev_dt_bias, ev_a_log, ev_d_skip, ev_ssd_norm, ev_q_norm, ev_k_norm, ev_sinks, ev_w_out, od_mix, od_w0, od_w1, od_w2, od_a0, od_a1, od_a2, od_g1, od_g2, od_k_k, od_k_a, od_r_k, od_w_r, od_w_k, od_w_v, od_w_o, od_ln_w, od_ln_b):
    batch, seq, d = x.shape
    depth = norm_mix.shape[0]
    inv_freq = ROPE_THETA ** (-jnp.arange(0, HEAD_DIM, 2, dtype=F32) / HEAD_DIM)
    ang = jnp.arange(seq, dtype=F32)[:, None] * inv_freq[None, :]
    cos, sin = jnp.cos(ang), jnp.sin(ang)
    cos_t = jnp.concatenate([cos, cos, cos, cos], axis=1)
    sin_t = jnp.concatenate([-sin, sin, -sin, sin], axis=1)
    xf = x.reshape(batch * seq, d)
    for i in range(depth):
        j = i // 2
        if i % 2 == 0:
            xf = _attn_ssd_layer(xf, batch, seq, norm_mix[i], ev_w_in[j], ev_conv_w[j], ev_conv_b[j],
                                 ev_dt_bias[j], ev_a_log[j], ev_d_skip[j], ev_ssd_norm[j], ev_q_norm[j],
                                 ev_k_norm[j], ev_sinks[j], ev_w_out[j], cos_t, sin_t)
        else:
            xf = _rwkv_layer(xf, batch, seq, norm_mix[i], od_mix[j], od_w0[j], od_w1[j], od_w2[j], od_a0[j],
                             od_a1[j], od_a2[j], od_g1[j], od_g2[j], od_k_k[j], od_k_a[j], od_r_k[j],
                             od_w_r[j], od_w_k[j], od_w_v[j], od_w_o[j], od_ln_w[j], od_ln_b[j])
        xf = mlp_block(xf, norm_ffn[i][None, :], w_up[i].astype(BF16), w_down[i].astype(BF16),
                       tm=_rows(xf, TM_MLP), tf=TF_MLP)
    return xf.reshape(batch, seq, d)
```

```python
import functools

import jax
import jax.numpy as jnp
from jax import lax
from jax.experimental import pallas as pl
from jax.experimental.pallas import tpu as pltpu

F32 = jnp.float32
BF16 = jnp.bfloat16

D_MODEL = 2048
HEAD_DIM = 64
N_Q_HEADS = 16
N_KV_HEADS = 4
Q_REP = N_Q_HEADS // N_KV_HEADS
WINDOW = 128
ROPE_THETA = 10000.0
SSM_HEADS = 16
SSM_HEAD_DIM = 64
D_INNER = 1024
D_STATE = 128
N_GROUPS = 2
CONV_WIDTH = 4
CHUNK = 128
RWKV_HEADS = 32
D_FF = 4 * D_MODEL
NORM_EPS = 1e-6
GN_EPS = 64e-5
Q_W = 1024
KV_W = 256
BC_W = 256

LANES = 128
SUBLANES = 8
VMEM_LIMIT_BYTES = 56 * 1024 * 1024

PROJ_W = 3 * 1024 + 4 * 256 + LANES
RWKV_CHUNK = 64


TM_PROJ = 1024
TM_OUT = 512
TM_BMM = 1024
TM_PRE = 256
TM_MLP = 1024
TF_MLP = 512


def _rows(x, want):
    return min(want, x.shape[0])


def _cparams(*sem):
    return pltpu.CompilerParams(dimension_semantics=sem, vmem_limit_bytes=VMEM_LIMIT_BYTES)


def _rms(x, g):
    inv = lax.rsqrt(jnp.mean(x * x, axis=-1, keepdims=True) + NORM_EPS)
    return x * inv * g


def _silu(x):
    return x * (1.0 / (1.0 + jnp.exp(-x)))


def _sigmoid(x):
    return 1.0 / (1.0 + jnp.exp(-x))


def _softplus(x):
    return jnp.maximum(x, 0.0) + jnp.log1p(jnp.exp(-jnp.abs(x)))


def _dot(a, b):
    return jnp.dot(a, b, preferred_element_type=F32)


def _dot_nt(a, b):
    return lax.dot_general(a, b, (((1,), (1,)), ((), ())), preferred_element_type=F32)


def _dot_tn(a, b):
    return lax.dot_general(a, b, (((0,), (0,)), ((), ())), preferred_element_type=F32)


def _split3(x):
    hi = x.astype(BF16)
    r1 = x - hi.astype(F32)
    mid = r1.astype(BF16)
    lo = (r1 - mid.astype(F32)).astype(BF16)
    return hi, mid, lo


def _dot_exact_rhs(sel_b, x):
    hi, mid, lo = _split3(x)
    return _dot(sel_b, hi) + _dot(sel_b, mid) + _dot(sel_b, lo)


def _dot_exact_lhs(x, sel_b):
    hi, mid, lo = _split3(x)
    return _dot(hi, sel_b) + _dot(mid, sel_b) + _dot(lo, sel_b)


def _norm_matmul_kernel(x_ref, g_ref, w_ref, o_ref, tail_ref, h_ref):
    @pl.when(pl.program_id(1) == 0)
    def _():
        h_ref[...] = _rms(x_ref[...], g_ref[...]).astype(BF16)

    acc = _dot(h_ref[...], w_ref[...])
    o_ref[...] = acc.astype(o_ref.dtype)

    @pl.when(pl.program_id(1) == pl.num_programs(1) - 1)
    def _():
        tail_ref[...] = acc[:, acc.shape[1] - LANES:]


def norm_matmul(x, g, w, *, tm, tn):
    t, d = x.shape
    n = w.shape[1]
    return pl.pallas_call(
        _norm_matmul_kernel,
        out_shape=(jax.ShapeDtypeStruct((t, n), BF16), jax.ShapeDtypeStruct((t, LANES), F32)),
        grid=(t // tm, n // tn),
        in_specs=[pl.BlockSpec((tm, d), lambda i, j: (i, 0)),
                  pl.BlockSpec((1, d), lambda i, j: (0, 0)),
                  pl.BlockSpec((d, tn), lambda i, j: (0, j))],
        out_specs=(pl.BlockSpec((tm, tn), lambda i, j: (i, j)), pl.BlockSpec((tm, LANES), lambda i, j: (i, 0))),
        scratch_shapes=[pltpu.VMEM((tm, d), BF16)],
        compiler_params=_cparams("parallel", "arbitrary"),
        name="norm_matmul",
    )(x, g, w)


def _matmul_residual_kernel(*refs, n_in):
    res_ref = refs[0]
    a_refs = refs[1:1 + n_in]
    w_refs = refs[1 + n_in:1 + 2 * n_in]
    o_ref = refs[1 + 2 * n_in]
    acc = res_ref[...]
    for a_ref, w_ref in zip(a_refs, w_refs):
        acc = acc + _dot(a_ref[...], w_ref[...])
    o_ref[...] = acc


def matmul_residual(res, a_list, w_list, *, tm, tn):
    t, n = res.shape
    n_in = len(a_list)
    in_specs = [pl.BlockSpec((tm, tn), lambda i, j: (i, j))]
    in_specs += [pl.BlockSpec((tm, a.shape[1]), lambda i, j: (i, 0)) for a in a_list]
    in_specs += [pl.BlockSpec((w.shape[0], tn), lambda i, j: (0, j)) for w in w_list]
    return pl.pallas_call(
        functools.partial(_matmul_residual_kernel, n_in=n_in),
        out_shape=jax.ShapeDtypeStruct((t, n), F32),
        grid=(t // tm, n // tn),
        in_specs=in_specs,
        out_specs=pl.BlockSpec((tm, tn), lambda i, j: (i, j)),
        compiler_params=_cparams("parallel", "arbitrary"),
        name="matmul_residual",
    )(res, *a_list, *w_list)


def _mlp_kernel(x_ref, g_ref, wu_ref, wd_ref, o_ref, h_ref):
    @pl.when(pl.program_id(1) == 0)
    def _():
        x = x_ref[...]
        h_ref[...] = _rms(x, g_ref[...]).astype(BF16)
        o_ref[...] = x

    u = _dot(h_ref[...], wu_ref[...])
    a = jnp.square(jnp.maximum(u, 0.0)).astype(BF16)
    o_ref[...] += _dot(a, wd_ref[...])


def mlp_block(x, g, w_up, w_down, layer, *, tm, tf):
    t, d = x.shape
    f = w_up.shape[2]
    return pl.pallas_call(
        _mlp_kernel,
        out_shape=jax.ShapeDtypeStruct((t, d), F32),
        grid=(t // tm, f // tf),
        in_specs=[pl.BlockSpec((tm, d), lambda i, j: (i, 0)),
                  pl.BlockSpec((1, d), lambda i, j: (0, 0)),
                  pl.BlockSpec((None, d, tf), lambda i, j: (layer, 0, j)),
                  pl.BlockSpec((None, tf, d), lambda i, j: (layer, j, 0))],
        out_specs=pl.BlockSpec((tm, d), lambda i, j: (i, 0)),
        scratch_shapes=[pltpu.VMEM((tm, d), BF16)],
        compiler_params=_cparams("parallel", "arbitrary"),
        name="mlp_block",
    )(x, g, w_up, w_down)


def _low_lanes(shape):
    return lax.broadcasted_iota(jnp.int32, shape, 1) % LANES < HEAD_DIM


def _block_rows(arr):
    low = _low_lanes(arr.shape)
    zero = jnp.zeros_like(arr)
    return jnp.concatenate([jnp.where(low, arr, zero), jnp.where(low, zero, arr)], axis=0)


def _head_sums(blk):
    low = _low_lanes(blk.shape)
    s_lo = jnp.sum(jnp.where(low, blk, 0.0), axis=-1, keepdims=True)
    s_hi = jnp.sum(jnp.where(low, 0.0, blk), axis=-1, keepdims=True)
    return jnp.where(low, s_lo, s_hi)


def _both_halves(pair):
    swapped = pltpu.roll(pair, HEAD_DIM, axis=1)
    low = _low_lanes(pair.shape)
    return jnp.where(low, pair, swapped), jnp.where(low, swapped, pair)


def _norm_rope(xp, gain, cos, sin):
    half = HEAD_DIM // 2
    ms = _head_sums(xp * xp) * (1.0 / HEAD_DIM)
    xn = xp * lax.rsqrt(ms + NORM_EPS) * gain
    from_below = pltpu.roll(xn, half, axis=1)
    from_above = pltpu.roll(xn, LANES - half, axis=1)
    first_half = lax.broadcasted_iota(jnp.int32, xp.shape, 1) % HEAD_DIM < half
    return xn * cos + jnp.where(first_half, from_above, from_below) * sin


def _attn_kernel(sinks_ref, q_ref, kc_ref, kp_ref, vc_ref, vp_ref, cc_ref, sc_ref, cp_ref, sp_ref,
                 qn_ref, kn_ref, o_ref):
    n = pl.program_id(1)
    w = WINDOW
    cos_c, sin_c = cc_ref[...], sc_ref[...]
    cos_pc = jnp.concatenate([cp_ref[...], cos_c], axis=0)
    sin_pc = jnp.concatenate([sp_ref[...], sin_c], axis=0)
    qn, kn = qn_ref[...], kn_ref[...]

    rows = lax.broadcasted_iota(jnp.int32, (Q_REP * w, 2 * w), 0)
    qi = rows % w
    kj = lax.broadcasted_iota(jnp.int32, (Q_REP * w, 2 * w), 1)
    lo = jnp.where(n == 0, w - 1, 0)
    mask = (kj > qi) & (kj <= qi + w) & (kj > lo)

    k_dup, v_dup = [], []
    for j in range(N_KV_HEADS // 2):
        ps = slice(j * LANES, (j + 1) * LANES)
        k_raw = jnp.concatenate([kp_ref[:, ps], kc_ref[:, ps]], axis=0).astype(F32)
        v_raw = jnp.concatenate([vp_ref[:, ps], vc_ref[:, ps]], axis=0).astype(F32)
        k_dup += [t.astype(BF16) for t in _both_halves(_norm_rope(k_raw, kn, cos_pc, sin_pc))]
        v_dup += [t.astype(BF16) for t in _both_halves(v_raw)]
    q_rows = []
    for i in range(N_Q_HEADS // 2):
        qp = _norm_rope(q_ref[:, i * LANES:(i + 1) * LANES].astype(F32), qn, cos_c, sin_c) * (HEAD_DIM ** -0.5)
        q_rows.append(_block_rows(qp.astype(BF16)))
    s = [_dot_nt(jnp.concatenate(q_rows[2 * g:2 * g + 2], axis=0), k_dup[g]) for g in range(N_KV_HEADS)]
    p, denom = [], []
    for g in range(N_KV_HEADS):
        sink = jnp.concatenate([jnp.full((w, 1), sinks_ref[g * Q_REP + r], F32) for r in range(Q_REP)], axis=0)
        sg = jnp.where(mask, s[g], -jnp.inf)
        m = jnp.maximum(jnp.max(sg, axis=-1, keepdims=True), sink)
        e = jnp.exp(sg - m)
        denom.append(jnp.sum(e, axis=-1, keepdims=True) + jnp.exp(sink - m))
        p.append(e.astype(BF16))
    o = [_dot(p[g], v_dup[g]) / denom[g] for g in range(N_KV_HEADS)]
    low = _low_lanes((w, LANES))
    for i in range(N_Q_HEADS // 2):
        og = o[i // 2]
        r0 = (i % 2) * 2 * w
        o_ref[:, i * LANES:(i + 1) * LANES] = jnp.where(low, og[r0:r0 + w], og[r0 + w:r0 + 2 * w]).astype(o_ref.dtype)


def swa_attention(proj, cos_t, sin_t, q_norm, k_norm, sinks, *, batch, seq):
    nb = seq // WINDOW
    kcol = 3 * 1024 // KV_W
    vcol = kcol + 1

    def cur(b, n):
        return b * nb + n

    def prev(b, n):
        return b * nb + jnp.maximum(n - 1, 0)

    return pl.pallas_call(
        _attn_kernel,
        out_shape=jax.ShapeDtypeStruct((batch * seq, Q_W), BF16),
        grid=(batch, nb),
        in_specs=[pl.BlockSpec(memory_space=pltpu.SMEM),
                  pl.BlockSpec((WINDOW, Q_W), lambda b, n: (cur(b, n), 0)),
                  pl.BlockSpec((WINDOW, KV_W), lambda b, n: (cur(b, n), kcol)),
                  pl.BlockSpec((WINDOW, KV_W), lambda b, n: (prev(b, n), kcol)),
                  pl.BlockSpec((WINDOW, KV_W), lambda b, n: (cur(b, n), vcol)),
                  pl.BlockSpec((WINDOW, KV_W), lambda b, n: (prev(b, n), vcol)),
                  pl.BlockSpec((WINDOW, LANES), lambda b, n: (n, 0)),
                  pl.BlockSpec((WINDOW, LANES), lambda b, n: (n, 0)),
                  pl.BlockSpec((WINDOW, LANES), lambda b, n: (jnp.maximum(n - 1, 0), 0)),
                  pl.BlockSpec((WINDOW, LANES), lambda b, n: (jnp.maximum(n - 1, 0), 0)),
                  pl.BlockSpec((1, LANES), lambda b, n: (0, 0)),
                  pl.BlockSpec((1, LANES), lambda b, n: (0, 0))],
        out_specs=pl.BlockSpec((WINDOW, Q_W), lambda b, n: (cur(b, n), 0)),
        compiler_params=_cparams("parallel", "parallel"),
        name="swa_attention",
    )(sinks, proj, proj, proj, proj, proj, cos_t, sin_t, cos_t, sin_t, q_norm, k_norm)


def _ssd_kernel(z_ref, x_ref, b_ref, c_ref, dt_ref, cwx_ref, cwb_ref, cwc_ref, cbx_ref, cbb_ref, cbc_ref,
                dtb_ref, alog_ref, dsk_ref, nrm_ref, exp_ref, o_ref,
                xpx_ref, xpb_ref, xpc_ref, state_ref, y_ref):
    q = CHUNK
    hist = SUBLANES

    @pl.when(pl.program_id(1) == 0)
    def _():
        xpx_ref[0:hist, :] = jnp.zeros((hist, D_INNER), F32)
        xpb_ref[0:hist, :] = jnp.zeros((hist, BC_W), F32)
        xpc_ref[0:hist, :] = jnp.zeros((hist, BC_W), F32)
        state_ref[...] = jnp.zeros_like(state_ref)

    def conv(raw_ref, pad_ref, cw_ref, cb_ref):
        pad_ref[hist:hist + q, :] = raw_ref[...].astype(F32)
        acc = cb_ref[...]
        for j in range(CONV_WIDTH):
            start = hist - (CONV_WIDTH - 1) + j
            acc = acc + pad_ref[start:start + q, :] * cw_ref[j:j + 1, :]
        pad_ref[0:hist, :] = pad_ref[q:q + hist, :]
        return _silu(acc)

    xs = conv(x_ref, xpx_ref, cwx_ref, cbx_ref)
    bm = conv(b_ref, xpb_ref, cwb_ref, cbb_ref)
    cm = conv(c_ref, xpc_ref, cwc_ref, cbc_ref)

    dt = _softplus(dt_ref[...] + dtb_ref[...])
    dta = dt * (-jnp.exp(alog_ref[...]))
    ri = lax.broadcasted_iota(jnp.int32, (q, q), 0)
    ci = lax.broadcasted_iota(jnp.int32, (q, q), 1)
    causal = ri >= ci
    tril = jnp.where(causal, 1.0, 0.0).astype(BF16)
    cs = _dot_exact_rhs(tril, dta)
    cs_t = cs.T
    cs_last = cs[q - 1:q, :]
    expand = exp_ref[...]
    stacked = jnp.concatenate([dt, jnp.exp(cs), jnp.exp(cs_last - cs)], axis=0).astype(BF16)
    wide = _dot(stacked, expand)
    dt_w, ecs_w, dout_w = wide[0:q], wide[q:2 * q], wide[2 * q:3 * q]
    cdec_w = _dot_exact_lhs(jnp.broadcast_to(jnp.exp(cs_last), (SUBLANES, LANES)), expand)[0:1]

    xdt = xs * dt_w
    xdt_b = xdt.astype(BF16)
    xdec_b = (xdt * dout_w).astype(BF16)
    hg = SSM_HEADS // N_GROUPS
    gw = hg * SSM_HEAD_DIM
    for g in range(N_GROUPS):
        cg = cm[:, g * D_STATE:(g + 1) * D_STATE].astype(BF16)
        bg = bm[:, g * D_STATE:(g + 1) * D_STATE].astype(BF16)
        cb = _dot_nt(cg, bg)
        gs = slice(g * gw, (g + 1) * gw)
        prev = state_ref[:, gs]
        y_ref[:, gs] = _dot(cg, prev.astype(BF16)) * ecs_w[:, gs]
        for hh in range(hg):
            h = g * hg + hh
            seg = cs[:, h:h + 1] - cs_t[h:h + 1, :]
            lm = jnp.exp(jnp.where(causal, seg, -jnp.inf))
            hs = slice(h * SSM_HEAD_DIM, (h + 1) * SSM_HEAD_DIM)
            y_ref[:, hs] += _dot((cb * lm).astype(BF16), xdt_b[:, hs])
        state_ref[:, gs] = prev * cdec_w[:, gs] + _dot_tn(bg, xdec_b[:, gs])

    y = y_ref[...] + dsk_ref[...] * xs
    y = y * _silu(z_ref[...].astype(F32))
    for g in range(N_GROUPS):
        gs = slice(g * gw, (g + 1) * gw)
        yg = y[:, gs]
        inv = lax.rsqrt(jnp.mean(yg * yg, axis=-1, keepdims=True) + NORM_EPS)
        o_ref[:, gs] = (yg * inv * nrm_ref[:, gs]).astype(o_ref.dtype)


def ssd_mixer(proj, dt_raw, conv_w, conv_b, dt_bias, a_log, d_skip, ssd_norm, *, batch, seq):
    nc = seq // CHUNK
    pad = LANES - SSM_HEADS
    cwx, cwb, cwc = conv_w[:, :D_INNER], conv_w[:, D_INNER:D_INNER + BC_W], conv_w[:, D_INNER + BC_W:]
    cbx, cbb, cbc = (conv_b[None, :D_INNER], conv_b[None, D_INNER:D_INNER + BC_W], conv_b[None, D_INNER + BC_W:])
    dtb = jnp.pad(dt_bias, (0, pad))[None, :]
    alog = jnp.pad(a_log, (0, pad))[None, :]
    dsk = jnp.repeat(d_skip, SSM_HEAD_DIM)[None, :]
    head_of_channel = jnp.arange(D_INNER) // SSM_HEAD_DIM
    expand = (jnp.arange(LANES)[:, None] == head_of_channel[None, :]).astype(BF16)

    def row(b, c):
        return b * nc + c

    def full(shape):
        return pl.BlockSpec(shape, lambda b, c: (0, 0))

    hist = SUBLANES
    return pl.pallas_call(
        _ssd_kernel,
        out_shape=jax.ShapeDtypeStruct((batch * seq, D_INNER), BF16),
        grid=(batch, nc),
        in_specs=[pl.BlockSpec((CHUNK, D_INNER), lambda b, c: (row(b, c), 1)),
                  pl.BlockSpec((CHUNK, D_INNER), lambda b, c: (row(b, c), 2)),
                  pl.BlockSpec((CHUNK, BC_W), lambda b, c: (row(b, c), 14)),
                  pl.BlockSpec((CHUNK, BC_W), lambda b, c: (row(b, c), 15)),
                  pl.BlockSpec((CHUNK, LANES), lambda b, c: (row(b, c), 0)),
                  full((CONV_WIDTH, D_INNER)), full((CONV_WIDTH, BC_W)), full((CONV_WIDTH, BC_W)),
                  full((1, D_INNER)), full((1, BC_W)), full((1, BC_W)),
                  full((1, LANES)), full((1, LANES)), full((1, D_INNER)), full((1, D_INNER)),
                  full((LANES, D_INNER))],
        out_specs=pl.BlockSpec((CHUNK, D_INNER), lambda b, c: (row(b, c), 0)),
        scratch_shapes=[pltpu.VMEM((CHUNK + hist, D_INNER), F32),
                        pltpu.VMEM((CHUNK + hist, BC_W), F32),
                        pltpu.VMEM((CHUNK + hist, BC_W), F32),
                        pltpu.VMEM((D_STATE, D_INNER), F32),
                        pltpu.VMEM((CHUNK, D_INNER), F32)],
        compiler_params=_cparams("parallel", "arbitrary"),
        name="ssd_mixer",
    )(proj, proj, proj, proj, dt_raw, cwx, cwb, cwc, cbx, cbb, cbc, dtb, alog, dsk, ssd_norm[None, :], expand)


def _rwkv_pre_kernel(x_ref, xp_ref, g_ref, mix_ref, w1_ref, a1_ref, g1_ref, w2_ref, a2_ref, g2_ref, w0_ref, a0_ref,
                     o_ref, ld_ref, a_ref, gate_ref, hp_ref, *, tm, seq):
    i = pl.program_id(0)
    hist = SUBLANES
    g = g_ref[...]
    h = _rms(x_ref[...], g)
    keep = jnp.where((i * tm) % seq == 0, 0.0, 1.0)
    hp_ref[0:hist, :] = _rms(xp_ref[...], g) * keep
    hp_ref[hist:hist + tm, :] = h
    xx = hp_ref[hist - 1:hist - 1 + tm, :] - h

    def mixed(m):
        return (h + xx * mix_ref[m:m + 1, :]).astype(BF16)

    for m in range(3):
        o_ref[m] = mixed(m)
    tw = jnp.tanh(_dot(mixed(3), w1_ref[...])).astype(BF16)
    ta = _dot(mixed(4), a1_ref[...]).astype(BF16)
    tg = _sigmoid(_dot(mixed(5), g1_ref[...])).astype(BF16)
    zw = w0_ref[...] + _dot(tw, w2_ref[...])
    ld_ref[...] = -jnp.exp(-_softplus(-zw) - 0.5)
    a_ref[...] = _sigmoid(a0_ref[...] + _dot(ta, a2_ref[...])).astype(a_ref.dtype)
    gate_ref[...] = _dot(tg, g2_ref[...]).astype(gate_ref.dtype)


def rwkv_pre(x, g, mix, w1, a1, g1, w2, a2, g2, w0, a0, *, tm, seq):
    t, d = x.shape
    hist = SUBLANES

    def full(a):
        return pl.BlockSpec(a.shape, lambda i: (0, 0))

    def rows():
        return pl.BlockSpec((tm, d), lambda i: (i, 0))

    return pl.pallas_call(
        functools.partial(_rwkv_pre_kernel, tm=tm, seq=seq),
        out_shape=(jax.ShapeDtypeStruct((3, t, d), BF16), jax.ShapeDtypeStruct((t, d), F32),
                   jax.ShapeDtypeStruct((t, d), BF16), jax.ShapeDtypeStruct((t, d), BF16)),
        grid=(t // tm,),
        in_specs=[rows(),
                  pl.BlockSpec((hist, d), lambda i: (jnp.maximum(i * (tm // hist) - 1, 0), 0)),
                  full(g), full(mix), full(w1), full(a1), full(g1), full(w2), full(a2), full(g2), full(w0), full(a0)],
        out_specs=(pl.BlockSpec((3, tm, d), lambda i: (0, i, 0)), rows(), rows(), rows()),
        scratch_shapes=[pltpu.VMEM((tm + hist, d), F32)],
        compiler_params=_cparams("parallel"),
        name="rwkv_pre",
    )(x, x, g, mix, w1, a1, g1, w2, a2, g2, w0, a0)


def _bmm_kernel(x_ref, w_ref, o_ref):
    o_ref[...] = _dot(x_ref[...], w_ref[...]).astype(o_ref.dtype)


def batched_matmul(xs, ws, *, tm, tn):
    nm, d, n = ws.shape
    t = xs.shape[1]
    return pl.pallas_call(
        _bmm_kernel,
        out_shape=jax.ShapeDtypeStruct((nm, t, n), BF16),
        grid=(nm, t // tm, n // tn),
        in_specs=[pl.BlockSpec((None, tm, d), lambda m, i, j: (m, i, 0)),
                  pl.BlockSpec((None, d, tn), lambda m, i, j: (m, 0, j))],
        out_specs=pl.BlockSpec((None, tm, tn), lambda m, i, j: (m, i, j)),
        compiler_params=_cparams("parallel", "parallel", "arbitrary"),
        name="batched_matmul",
    )(xs, ws)


def _rwkv_scan_kernel(r_ref, k_ref, v_ref, ld_ref, a_ref, g_ref, kk_ref, ka_ref, rk_ref, lnw_ref, lnb_ref,
                      o_ref, s_ref, *, heads, chunks):
    c = RWKV_CHUNK
    n = HEAD_DIM

    @pl.when(pl.program_id(2) == 0)
    def _():
        s_ref[...] = jnp.zeros_like(s_ref)

    ri = lax.broadcasted_iota(jnp.int32, (c, c), 0)
    ci = lax.broadcasted_iota(jnp.int32, (c, c), 1)
    tril = jnp.where(ri >= ci, 1.0, 0.0).astype(BF16)
    ri2 =lax.broadcasted_iota(jnp.int32, (2 * c, 4 * c), 0)
    ci2 = lax.broadcasted_iota(jnp.int32, (2 * c, 4 * c), 1) % c
    gmask = ci2 <= jnp.where(ri2 < c, ri2 - 1, ri2 - c)
    zeros_sq = jnp.zeros((LANES, LANES), BF16)
    n_steps = c.bit_length() - 1
    pairs = heads // 2

    block_rows, head_sums = _block_rows, _head_sums

    def all_head_sums(arr):
        return jnp.concatenate([head_sums(arr[:, j * LANES:(j + 1) * LANES]) for j in range(pairs)], axis=1)

    block_diag = (lax.broadcasted_iota(jnp.int32, (LANES, LANES), 0) < n) == _low_lanes((LANES, LANES))

    items = [(ch, j) for ch in range(chunks) for j in range(pairs)]
    at_p, rt_p, v_p, bd_v, bk_t, w_t, bonus, lhs1, rhs1 = ({} for _ in range(9))
    for ch in range(chunks):
        rows = slice(ch * c, (ch + 1) * c)
        ld = ld_ref[rows, :]
        cl = _dot_exact_rhs(tril, ld)
        cl_last = cl[c - 1:c, :]
        e_inc = jnp.exp(cl)
        e_neg = jnp.exp(-cl)
        e_exc = jnp.exp(cl - ld)
        e_bar = jnp.exp(cl_last - cl)
        w_all = jnp.exp(cl_last)
        r_all, k_all, v_all, a_all = (t_ref[rows, :].astype(F32) for t_ref in (r_ref, k_ref, v_ref, a_ref))
        kk_all = k_all * kk_ref[...]
        kk_all = kk_all * lax.rsqrt(jnp.maximum(all_head_sums(kk_all * kk_all), 1e-24))
        kf_all = k_all * (1.0 + (a_all - 1.0) * ka_ref[...])
        bb_all = kk_all * a_all
        bonus[ch] = all_head_sums(r_all * kf_all * rk_ref[...]) * v_all
        at_all = (-kk_all * e_exc).astype(BF16)
        rt_all = (r_all * e_inc).astype(BF16)
        bt_all = (bb_all * e_neg).astype(BF16)
        kt_all = (kf_all * e_neg).astype(BF16)
        vb_all = v_all.astype(BF16)
        bbar_all = bb_all * e_bar
        kbar_all = kf_all * e_bar
        w_rows = jnp.broadcast_to(w_all, (2 * c, heads * n))
        for j in range(pairs):
            it = (ch, j)
            ps = slice(j * LANES, (j + 1) * LANES)
            at_p[it], rt_p[it], v_p[it] = at_all[:, ps], rt_all[:, ps], vb_all[:, ps]
            lhs1[it] = jnp.concatenate([at_p[it], rt_p[it]], axis=0)
            rhs1[it] = jnp.concatenate([block_rows(bt_all[:, ps]), block_rows(kt_all[:, ps])], axis=0)
            bd_v[it] = block_rows(v_p[it])
            bk_t[it] = jnp.concatenate([bbar_all[:, ps], kbar_all[:, ps]], axis=0).T.astype(BF16)
            w_t[it] = w_rows[:, ps].T

    gm = {it: jnp.where(gmask, _dot_nt(lhs1[it], rhs1[it]), 0.0) for it in items}
    a_r = {it: gm[it][c:2 * c, :].astype(BF16) for it in items}
    p_b = {it: gm[it][0:c, 0:LANES].astype(BF16) for it in items}
    akv = {it: _dot(gm[it][0:c, LANES:2 * LANES].astype(BF16), bd_v[it]) for it in items}
    x = {it: jnp.concatenate([at_p[it].astype(F32), akv[it]], axis=1) for it in items}
    for step in range(n_steps):
        x = {it: x[it] + _dot(p_b[it], block_rows(x[it].astype(BF16))) for it in items}
        if step + 1 < n_steps:
            p_b = {it: _dot(p_b[it], block_rows(p_b[it])).astype(BF16) for it in items}
    top = {}
    for it in items:
        rhs = jnp.concatenate([block_rows(x[it].astype(BF16)),
                               jnp.concatenate([zeros_sq, bd_v[it]], axis=1)], axis=0)
        top[it] = _dot(a_r[it], rhs)
    lhs5 = {it: jnp.concatenate([x[it][:, 0:LANES], rt_p[it].astype(F32) + top[it][:, 0:LANES]],
                                axis=0).astype(BF16) for it in items}

    for ch in range(chunks):
        rows = slice(ch * c, (ch + 1) * c)
        z0 = [s_ref[j] for j in range(pairs)]
        inter = [_dot(lhs5[(ch, j)], z0[j].astype(BF16)) for j in range(pairs)]
        for j in range(pairs):
            it = (ch, j)
            ps = slice(j * LANES, (j + 1) * LANES)
            u = inter[j][0:c] + x[it][:, LANES:2 * LANES]
            y = inter[j][c:2 * c] + top[it][:, LANES:2 * LANES]
            uv = jnp.concatenate([u.astype(BF16), v_p[it]], axis=0)
            s_ref[j] = z0[j] * w_t[it] + jnp.where(block_diag, _dot(bk_t[it], uv), 0.0)
            yc = y - head_sums(y) * (1.0 / n)
            yn = yc * lax.rsqrt(head_sums(yc * yc) * (1.0 / n) + GN_EPS)
            out = (yn * lnw_ref[:, ps] + lnb_ref[:, ps] + bonus[ch][:, ps]) * g_ref[rows, ps].astype(F32)
            o_ref[rows, ps] = out.astype(o_ref.dtype)


def rwkv_scan(rkv, ld, a, g, k_k, k_a, r_k, ln_w, ln_b, *, batch, seq, heads, chunks):
    t, d = ld.shape
    rows = chunks * RWKV_CHUNK
    nblk = seq // rows
    width = heads * HEAD_DIM

    def data(b, hg, c):
        return (b * nblk + c, hg)

    def stacked(m):
        return pl.BlockSpec((None, rows, width), lambda b, hg, c: (m, b * nblk + c, hg))

    def param():
        return pl.BlockSpec((1, width), lambda b, hg, c: (0, hg))

    return pl.pallas_call(
        functools.partial(_rwkv_scan_kernel, heads=heads, chunks=chunks),
        out_shape=jax.ShapeDtypeStruct((t, d), BF16),
        grid=(batch, d // width, nblk),
        in_specs=[stacked(0), stacked(1), stacked(2),
                  pl.BlockSpec((rows, width), data), pl.BlockSpec((rows, width), data),
                  pl.BlockSpec((rows, width), data),
                  param(), param(), param(), param(), param()],
        out_specs=pl.BlockSpec((rows, width), data),
        scratch_shapes=[pltpu.VMEM((heads // 2, LANES, LANES), F32)],
        compiler_params=_cparams("parallel", "parallel", "arbitrary"),
        name="rwkv_scan",
    )(rkv, rkv, rkv, ld, a, g, k_k[None, :], k_a[None, :], r_k.reshape(1, d), ln_w[None, :], ln_b[None, :])


def _attn_ssd_layer(x, batch, seq, norm_g, w_in, conv_w, conv_b, dt_bias, a_log, d_skip, ssd_norm,
                    q_norm, k_norm, sinks, w_out, cos_t, sin_t):
    q_end, k_end, v_end, z_end, xbc_end = 1024, 1280, 1536, 2560, 4096
    x_end = z_end + D_INNER
    w_in_p = jnp.concatenate(
        [w_in[:, :q_end], w_in[:, v_end:z_end], w_in[:, z_end:x_end], w_in[:, q_end:k_end], w_in[:, k_end:v_end],
         w_in[:, x_end:xbc_end], w_in[:, xbc_end:], jnp.zeros((D_MODEL, LANES - SSM_HEADS), w_in.dtype)],
        axis=1).astype(BF16)
    proj, dt_raw = norm_matmul(x, norm_g[None, :], w_in_p, tm=_rows(x, TM_PROJ), tn=PROJ_W // 3)
    y_attn = swa_attention(proj, cos_t, sin_t, jnp.tile(q_norm, 2)[None, :], jnp.tile(k_norm, 2)[None, :], sinks,
                           batch=batch, seq=seq)
    y_ssd = ssd_mixer(proj, dt_raw, conv_w, conv_b, dt_bias, a_log, d_skip, ssd_norm, batch=batch, seq=seq)
    w_out_b = w_out.astype(BF16)
    return matmul_residual(x, [y_attn, y_ssd], [w_out_b[:Q_W], w_out_b[Q_W:]], tm=_rows(x, TM_OUT), tn=D_MODEL)


def _rwkv_layer(x, batch, seq, norm_g, mix, w0, w1, w2, a0, a1, a2, g1, g2, k_k, k_a, r_k,
                w_r, w_k, w_v, w_o, ln_w, ln_b):
    mix_p = jnp.stack([mix[0], mix[2], mix[3], mix[1], mix[4], mix[5]])
    lora_pad = LANES - w1.shape[1]
    w1_p = jnp.pad(w1, ((0, 0), (0, lora_pad))).astype(BF16)
    a1_p = jnp.pad(a1, ((0, 0), (0, lora_pad))).astype(BF16)
    w2_p = jnp.pad(w2, ((0, lora_pad), (0, 0))).astype(BF16)
    a2_p = jnp.pad(a2, ((0, lora_pad), (0, 0))).astype(BF16)
    xs, ld, a, g = rwkv_pre(x, norm_g[None, :], mix_p, w1_p, a1_p, g1.astype(BF16), w2_p, a2_p, g2.astype(BF16),
                            w0[None, :], a0[None, :], tm=_rows(x, TM_PRE), seq=seq)
    rkv = batched_matmul(xs, jnp.stack([w_r, w_k, w_v]).astype(BF16), tm=_rows(x, TM_BMM), tn=D_MODEL)
    yg = rwkv_scan(rkv, ld, a, g, k_k, k_a, r_k, ln_w, ln_b, batch=batch, seq=seq, heads=16, chunks=4)
    return matmul_residual(x, [yg], [w_o.astype(BF16)], tm=_rows(x, TM_OUT), tn=D_MODEL)


def kernel(x, norm_mix, norm_ffn, w_up, w_down, ev_w_in, ev_conv_w, ev_conv_b, ev_dt_bias, ev_a_log, ev_d_skip, ev_ssd_norm, ev_q_norm, ev_k_norm, ev_sinks, ev_w_out, od_mix, od_w0, od_w1, od_w2, od_a0, od_a1, od_a2, od_g1, od_g2, od_k_k, od_k_a, od_r_k, od_w_r, od_w_k, od_w_v, od_w_o, od_ln_w, od_ln_b):
    batch, seq, d = x.shape
    depth = norm_mix.shape[0]
    inv_freq = ROPE_THETA ** (-jnp.arange(0, HEAD_DIM, 2, dtype=F32) / HEAD_DIM)
    ang = jnp.arange(seq, dtype=F32)[:, None] * inv_freq[None, :]
    cos, sin = jnp.cos(ang), jnp.sin(ang)
    cos_t = jnp.concatenate([cos, cos, cos, cos], axis=1)
    sin_t = jnp.concatenate([-sin, sin, -sin, sin], axis=1)
    xf = x.reshape(batch * seq, d)
    w_up_b, w_down_b = w_up.astype(BF16), w_down.astype(BF16)
    for i in range(depth):
        j = i // 2
        if i % 2 == 0:
            xf = _attn_ssd_layer(xf, batch, seq, norm_mix[i], ev_w_in[j], ev_conv_w[j], ev_conv_b[j],
                                 ev_dt_bias[j], ev_a_log[j], ev_d_skip[j], ev_ssd_norm[j], ev_q_norm[j],
                                 ev_k_norm[j], ev_sinks[j], ev_w_out[j], cos_t, sin_t)
        else:
            xf = _rwkv_layer(xf, batch, seq, norm_mix[i], od_mix[j], od_w0[j], od_w1[j], od_w2[j], od_a0[j],
                             od_a1[j], od_a2[j], od_g1[j], od_g2[j], od_k_k[j], od_k_a[j], od_r_k[j],
                             od_w_r[j], od_w_k[j], od_w_v[j], od_w_o[j], od_ln_w[j], od_ln_b[j])
        xf = mlp_block(xf, norm_ffn[i][None, :], w_up_b, w_down_b, i, tm=_rows(xf, TM_MLP), tf=TF_MLP)
    return xf.reshape(batch, seq, d)
```

```python
import functools

import jax
import jax.numpy as jnp
from jax import lax
from jax.experimental import pallas as pl
from jax.experimental.pallas import tpu as pltpu

F32 = jnp.float32
BF16 = jnp.bfloat16

D_MODEL = 2048
HEAD_DIM = 64
N_Q_HEADS = 16
N_KV_HEADS = 4
Q_REP = N_Q_HEADS // N_KV_HEADS
WINDOW = 128
ROPE_THETA = 10000.0
SSM_HEADS = 16
SSM_HEAD_DIM = 64
D_INNER = 1024
D_STATE = 128
N_GROUPS = 2
CONV_WIDTH = 4
CHUNK = 128
RWKV_HEADS = 32
D_FF = 4 * D_MODEL
NORM_EPS = 1e-6
GN_EPS = 64e-5
Q_W = 1024
KV_W = 256
BC_W = 256

LANES = 128
SUBLANES = 8
VMEM_LIMIT_BYTES = 56 * 1024 * 1024

DECAY_SCALE = 0.6065306597126334
RWKV_CHUNK = 64


TM_PROJ = 1024
TN_PROJ = 1024
TM_OUT = 512
TM_BMM = 1024
TM_PRE = 256
TM_MLP = 1024
TF_MLP = 512


def _rows(x, want):
    return min(want, x.shape[0])


def _cparams(*sem):
    return pltpu.CompilerParams(dimension_semantics=sem, vmem_limit_bytes=VMEM_LIMIT_BYTES)


def _rms(x, g):
    inv = lax.rsqrt(jnp.mean(x * x, axis=-1, keepdims=True) + NORM_EPS)
    return x * inv * g


def _silu(x):
    return x * (1.0 / (1.0 + jnp.exp(-x)))


def _sigmoid(x):
    return 1.0 / (1.0 + jnp.exp(-x))


def _softplus(x):
    return jnp.maximum(x, 0.0) + jnp.log1p(jnp.exp(-jnp.abs(x)))


def _dot(a, b):
    return jnp.dot(a, b, preferred_element_type=F32)


def _dot_nt(a, b):
    return lax.dot_general(a, b, (((1,), (1,)), ((), ())), preferred_element_type=F32)


def _dot_tn(a, b):
    return lax.dot_general(a, b, (((0,), (0,)), ((), ())), preferred_element_type=F32)


def _split3(x):
    hi = x.astype(BF16)
    r1 = x - hi.astype(F32)
    mid = r1.astype(BF16)
    lo = (r1 - mid.astype(F32)).astype(BF16)
    return hi, mid, lo


def _dot_exact_rhs(sel_b, x):
    hi, mid, lo = _split3(x)
    return _dot(sel_b, hi) + _dot(sel_b, mid) + _dot(sel_b, lo)


def _dot_exact_lhs(x, sel_b):
    hi, mid, lo = _split3(x)
    return _dot(hi, sel_b) + _dot(mid, sel_b) + _dot(lo, sel_b)


def _norm_matmul_kernel(x_ref, g_ref, w_ref, ws_ref, o_ref, side_ref, h_ref):
    @pl.when(pl.program_id(1) == 0)
    def _():
        h_ref[...] = _rms(x_ref[...], g_ref[...]).astype(BF16)
        side_ref[...] = _dot(h_ref[...], ws_ref[...])

    o_ref[...] = _dot(h_ref[...], w_ref[...]).astype(o_ref.dtype)


def norm_matmul(x, g, w, w_side, *, tm, tn):
    t, d = x.shape
    n = w.shape[1]
    ns = w_side.shape[1]
    return pl.pallas_call(
        _norm_matmul_kernel,
        out_shape=(jax.ShapeDtypeStruct((t, n), BF16), jax.ShapeDtypeStruct((t, ns), F32)),
        grid=(t // tm, n // tn),
        in_specs=[pl.BlockSpec((tm, d), lambda i, j: (i, 0)),
                  pl.BlockSpec((1, d), lambda i, j: (0, 0)),
                  pl.BlockSpec((d, tn), lambda i, j: (0, j)),
                  pl.BlockSpec((d, ns), lambda i, j: (0, 0))],
        out_specs=(pl.BlockSpec((tm, tn), lambda i, j: (i, j)), pl.BlockSpec((tm, ns), lambda i, j: (i, 0))),
        scratch_shapes=[pltpu.VMEM((tm, d), BF16)],
        compiler_params=_cparams("parallel", "arbitrary"),
        name="norm_matmul",
    )(x, g, w, w_side)


def _matmul_residual_kernel(*refs, n_in):
    res_ref = refs[0]
    a_refs = refs[1:1 + n_in]
    w_refs = refs[1 + n_in:1 + 2 * n_in]
    o_ref = refs[1 + 2 * n_in]
    acc = res_ref[...]
    for a_ref, w_ref in zip(a_refs, w_refs):
        acc = acc + _dot(a_ref[...], w_ref[...])
    o_ref[...] = acc


def matmul_residual(res, a_list, w_list, *, tm, tn):
    t, n = res.shape
    n_in = len(a_list)
    in_specs = [pl.BlockSpec((tm, tn), lambda i, j: (i, j))]
    in_specs += [pl.BlockSpec((tm, a.shape[1]), lambda i, j: (i, 0)) for a in a_list]
    in_specs += [pl.BlockSpec((w.shape[0], tn), lambda i, j: (0, j)) for w in w_list]
    return pl.pallas_call(
        functools.partial(_matmul_residual_kernel, n_in=n_in),
        out_shape=jax.ShapeDtypeStruct((t, n), F32),
        grid=(t // tm, n // tn),
        in_specs=in_specs,
        out_specs=pl.BlockSpec((tm, tn), lambda i, j: (i, j)),
        compiler_params=_cparams("parallel", "arbitrary"),
        name="matmul_residual",
    )(res, *a_list, *w_list)


def _mlp_kernel(x_ref, g_ref, wu_ref, wd_ref, o_ref, h_ref):
    @pl.when(pl.program_id(1) == 0)
    def _():
        x = x_ref[...]
        h_ref[...] = _rms(x, g_ref[...]).astype(BF16)
        o_ref[...] = x

    u = _dot(h_ref[...], wu_ref[...])
    a = jnp.square(jnp.maximum(u, 0.0)).astype(BF16)
    o_ref[...] += _dot(a, wd_ref[...])


def mlp_block(x, g, w_up, w_down, layer, *, tm, tf):
    t, d = x.shape
    f = w_up.shape[2]
    return pl.pallas_call(
        _mlp_kernel,
        out_shape=jax.ShapeDtypeStruct((t, d), F32),
        grid=(t // tm, f // tf),
        in_specs=[pl.BlockSpec((tm, d), lambda i, j: (i, 0)),
                  pl.BlockSpec((1, d), lambda i, j: (0, 0)),
                  pl.BlockSpec((None, d, tf), lambda i, j: (layer, 0, j)),
                  pl.BlockSpec((None, tf, d), lambda i, j: (layer, j, 0))],
        out_specs=pl.BlockSpec((tm, d), lambda i, j: (i, 0)),
        scratch_shapes=[pltpu.VMEM((tm, d), BF16)],
        compiler_params=_cparams("parallel", "arbitrary"),
        name="mlp_block",
    )(x, g, w_up, w_down)


def _low_lanes(shape):
    return lax.broadcasted_iota(jnp.int32, shape, 1) % LANES < HEAD_DIM


def _block_rows(arr):
    low = _low_lanes(arr.shape)
    zero = jnp.zeros_like(arr)
    return jnp.concatenate([jnp.where(low, arr, zero), jnp.where(low, zero, arr)], axis=0)


def _head_sums(blk):
    low = _low_lanes(blk.shape)
    s_lo = jnp.sum(jnp.where(low, blk, 0.0), axis=-1, keepdims=True)
    s_hi = jnp.sum(jnp.where(low, 0.0, blk), axis=-1, keepdims=True)
    return jnp.where(low, s_lo, s_hi)


def _both_halves(pair):
    swapped = pltpu.roll(pair, HEAD_DIM, axis=1)
    low = _low_lanes(pair.shape)
    return jnp.where(low, pair, swapped), jnp.where(low, swapped, pair)


def _norm_rope(xp, gain, cos, sin):
    half = HEAD_DIM // 2
    ms = _head_sums(xp * xp) * (1.0 / HEAD_DIM)
    xn = xp * lax.rsqrt(ms + NORM_EPS) * gain
    from_below = pltpu.roll(xn, half, axis=1)
    from_above = pltpu.roll(xn, LANES - half, axis=1)
    first_half = lax.broadcasted_iota(jnp.int32, xp.shape, 1) % HEAD_DIM < half
    return xn * cos + jnp.where(first_half, from_above, from_below) * sin


def _attn_kernel(sinks_ref, q_ref, kc_ref, kp_ref, vc_ref, vp_ref, cc_ref, sc_ref, cp_ref, sp_ref,
                 qn_ref, kn_ref, o_ref):
    n = pl.program_id(1)
    w = WINDOW
    cos_c, sin_c = cc_ref[...], sc_ref[...]
    cos_pc = jnp.concatenate([cp_ref[...], cos_c], axis=0)
    sin_pc = jnp.concatenate([sp_ref[...], sin_c], axis=0)
    qn, kn = qn_ref[...], kn_ref[...]

    rows = lax.broadcasted_iota(jnp.int32, (Q_REP * w, 2 * w), 0)
    qi = rows % w
    kj = lax.broadcasted_iota(jnp.int32, (Q_REP * w, 2 * w), 1)
    lo = jnp.where(n == 0, w - 1, 0)
    mask = (kj > qi) & (kj <= qi + w) & (kj > lo)

    k_dup, v_dup = [], []
    for j in range(N_KV_HEADS // 2):
        ps = slice(j * LANES, (j + 1) * LANES)
        k_raw = jnp.concatenate([kp_ref[:, ps], kc_ref[:, ps]], axis=0).astype(F32)
        v_raw = jnp.concatenate([vp_ref[:, ps], vc_ref[:, ps]], axis=0).astype(F32)
        k_dup += [t.astype(BF16) for t in _both_halves(_norm_rope(k_raw, kn, cos_pc, sin_pc))]
        v_dup += [t.astype(BF16) for t in _both_halves(v_raw)]
    q_rows = []
    for i in range(N_Q_HEADS // 2):
        qp = _norm_rope(q_ref[:, i * LANES:(i + 1) * LANES].astype(F32), qn, cos_c, sin_c) * (HEAD_DIM ** -0.5)
        q_rows.append(_block_rows(qp.astype(BF16)))
    s = [_dot_nt(jnp.concatenate(q_rows[2 * g:2 * g + 2], axis=0), k_dup[g]) for g in range(N_KV_HEADS)]
    p, denom = [], []
    for g in range(N_KV_HEADS):
        sink = jnp.concatenate([jnp.full((w, 1), sinks_ref[g * Q_REP + r], F32) for r in range(Q_REP)], axis=0)
        sg = jnp.where(mask, s[g], -jnp.inf)
        m = jnp.maximum(jnp.max(sg, axis=-1, keepdims=True), sink)
        e = jnp.exp(sg - m)
        denom.append(jnp.sum(e, axis=-1, keepdims=True) + jnp.exp(sink - m))
        p.append(e.astype(BF16))
    o = [_dot(p[g], v_dup[g]) / denom[g] for g in range(N_KV_HEADS)]
    low = _low_lanes((w, LANES))
    for i in range(N_Q_HEADS // 2):
        og = o[i // 2]
        r0 = (i % 2) * 2 * w
        o_ref[:, i * LANES:(i + 1) * LANES] = jnp.where(low, og[r0:r0 + w], og[r0 + w:r0 + 2 * w]).astype(o_ref.dtype)


def swa_attention(proj, cos_t, sin_t, q_norm, k_norm, sinks, *, batch, seq):
    nb = seq // WINDOW
    kcol = 3 * 1024 // KV_W
    vcol = kcol + 1

    def cur(b, n):
        return b * nb + n

    def prev(b, n):
        return b * nb + jnp.maximum(n - 1, 0)

    return pl.pallas_call(
        _attn_kernel,
        out_shape=jax.ShapeDtypeStruct((batch * seq, Q_W), BF16),
        grid=(batch, nb),
        in_specs=[pl.BlockSpec(memory_space=pltpu.SMEM),
                  pl.BlockSpec((WINDOW, Q_W), lambda b, n: (cur(b, n), 0)),
                  pl.BlockSpec((WINDOW, KV_W), lambda b, n: (cur(b, n), kcol)),
                  pl.BlockSpec((WINDOW, KV_W), lambda b, n: (prev(b, n), kcol)),
                  pl.BlockSpec((WINDOW, KV_W), lambda b, n: (cur(b, n), vcol)),
                  pl.BlockSpec((WINDOW, KV_W), lambda b, n: (prev(b, n), vcol)),
                  pl.BlockSpec((WINDOW, LANES), lambda b, n: (n, 0)),
                  pl.BlockSpec((WINDOW, LANES), lambda b, n: (n, 0)),
                  pl.BlockSpec((WINDOW, LANES), lambda b, n: (jnp.maximum(n - 1, 0), 0)),
                  pl.BlockSpec((WINDOW, LANES), lambda b, n: (jnp.maximum(n - 1, 0), 0)),
                  pl.BlockSpec((1, LANES), lambda b, n: (0, 0)),
                  pl.BlockSpec((1, LANES), lambda b, n: (0, 0))],
        out_specs=pl.BlockSpec((WINDOW, Q_W), lambda b, n: (cur(b, n), 0)),
        compiler_params=_cparams("parallel", "parallel"),
        name="swa_attention",
    )(sinks, proj, proj, proj, proj, proj, cos_t, sin_t, cos_t, sin_t, q_norm, k_norm)


def _ssd_kernel(z_ref, x_ref, b_ref, c_ref, dt_ref, cwx_ref, cwb_ref, cwc_ref, cbx_ref, cbb_ref, cbc_ref,
                dtb_ref, alog_ref, dsk_ref, nrm_ref, exp_ref, o_ref,
                xpx_ref, xpb_ref, xpc_ref, state_ref, y_ref):
    q = CHUNK
    hist = SUBLANES

    @pl.when(pl.program_id(1) == 0)
    def _():
        xpx_ref[0:hist, :] = jnp.zeros((hist, D_INNER), F32)
        xpb_ref[0:hist, :] = jnp.zeros((hist, BC_W), F32)
        xpc_ref[0:hist, :] = jnp.zeros((hist, BC_W), F32)
        state_ref[...] = jnp.zeros_like(state_ref)

    def conv(raw_ref, pad_ref, cw_ref, cb_ref):
        pad_ref[hist:hist + q, :] = raw_ref[...].astype(F32)
        acc = cb_ref[...]
        for j in range(CONV_WIDTH):
            start = hist - (CONV_WIDTH - 1) + j
            acc = acc + pad_ref[start:start + q, :] * cw_ref[j:j + 1, :]
        pad_ref[0:hist, :] = pad_ref[q:q + hist, :]
        return _silu(acc)

    xs = conv(x_ref, xpx_ref, cwx_ref, cbx_ref)
    bm = conv(b_ref, xpb_ref, cwb_ref, cbb_ref)
    cm = conv(c_ref, xpc_ref, cwc_ref, cbc_ref)

    dt = _softplus(dt_ref[...] + dtb_ref[...])
    dta = dt * (-jnp.exp(alog_ref[...]))
    ri = lax.broadcasted_iota(jnp.int32, (q, q), 0)
    ci = lax.broadcasted_iota(jnp.int32, (q, q), 1)
    causal = ri >= ci
    tril = jnp.where(causal, 1.0, 0.0).astype(BF16)
    cs = _dot_exact_rhs(tril, dta)
    cs_t = cs.T
    cs_last = cs[q - 1:q, :]
    expand = exp_ref[...]
    stacked = jnp.concatenate([dt, jnp.exp(cs), jnp.exp(cs_last - cs)], axis=0).astype(BF16)
    wide = _dot(stacked, expand)
    dt_w, ecs_w, dout_w = wide[0:q], wide[q:2 * q], wide[2 * q:3 * q]
    cdec_w = _dot_exact_lhs(jnp.broadcast_to(jnp.exp(cs_last), (SUBLANES, LANES)), expand)[0:1]

    xdt = xs * dt_w
    xdt_b = xdt.astype(BF16)
    xdec_b = (xdt * dout_w).astype(BF16)
    hg = SSM_HEADS // N_GROUPS
    gw = hg * SSM_HEAD_DIM
    for g in range(N_GROUPS):
        cg = cm[:, g * D_STATE:(g + 1) * D_STATE].astype(BF16)
        bg = bm[:, g * D_STATE:(g + 1) * D_STATE].astype(BF16)
        cb = _dot_nt(cg, bg)
        gs = slice(g * gw, (g + 1) * gw)
        prev = state_ref[:, gs]
        y_ref[:, gs] = _dot(cg, prev.astype(BF16)) * ecs_w[:, gs]
        for hh in range(hg):
            h = g * hg + hh
            seg = cs[:, h:h + 1] - cs_t[h:h + 1, :]
            lm = jnp.exp(jnp.where(causal, seg, -jnp.inf))
            hs = slice(h * SSM_HEAD_DIM, (h + 1) * SSM_HEAD_DIM)
            y_ref[:, hs] += _dot((cb * lm).astype(BF16), xdt_b[:, hs])
        state_ref[:, gs] = prev * cdec_w[:, gs] + _dot_tn(bg, xdec_b[:, gs])

    y = y_ref[...] + dsk_ref[...] * xs
    y = y * _silu(z_ref[...].astype(F32))
    for g in range(N_GROUPS):
        gs = slice(g * gw, (g + 1) * gw)
        yg = y[:, gs]
        inv = lax.rsqrt(jnp.mean(yg * yg, axis=-1, keepdims=True) + NORM_EPS)
        o_ref[:, gs] = (yg * inv * nrm_ref[:, gs]).astype(o_ref.dtype)


def ssd_mixer(proj, dt_raw, conv_w, conv_b, dt_bias, a_log, d_skip, ssd_norm, *, batch, seq):
    nc = seq // CHUNK
    pad = LANES - SSM_HEADS
    cwx, cwb, cwc = conv_w[:, :D_INNER], conv_w[:, D_INNER:D_INNER + BC_W], conv_w[:, D_INNER + BC_W:]
    cbx, cbb, cbc = (conv_b[None, :D_INNER], conv_b[None, D_INNER:D_INNER + BC_W], conv_b[None, D_INNER + BC_W:])
    dtb = jnp.pad(dt_bias, (0, pad))[None, :]
    alog = jnp.pad(a_log, (0, pad))[None, :]
    dsk = jnp.repeat(d_skip, SSM_HEAD_DIM)[None, :]
    head_of_channel = jnp.arange(D_INNER) // SSM_HEAD_DIM
    expand = (jnp.arange(LANES)[:, None] == head_of_channel[None, :]).astype(BF16)

    def row(b, c):
        return b * nc + c

    def full(shape):
        return pl.BlockSpec(shape, lambda b, c: (0, 0))

    hist = SUBLANES
    return pl.pallas_call(
        _ssd_kernel,
        out_shape=jax.ShapeDtypeStruct((batch * seq, D_INNER), BF16),
        grid=(batch, nc),
        in_specs=[pl.BlockSpec((CHUNK, D_INNER), lambda b, c: (row(b, c), 1)),
                  pl.BlockSpec((CHUNK, D_INNER), lambda b, c: (row(b, c), 2)),
                  pl.BlockSpec((CHUNK, BC_W), lambda b, c: (row(b, c), 14)),
                  pl.BlockSpec((CHUNK, BC_W), lambda b, c: (row(b, c), 15)),
                  pl.BlockSpec((CHUNK, LANES), lambda b, c: (row(b, c), 0)),
                  full((CONV_WIDTH, D_INNER)), full((CONV_WIDTH, BC_W)), full((CONV_WIDTH, BC_W)),
                  full((1, D_INNER)), full((1, BC_W)), full((1, BC_W)),
                  full((1, LANES)), full((1, LANES)), full((1, D_INNER)), full((1, D_INNER)),
                  full((LANES, D_INNER))],
        out_specs=pl.BlockSpec((CHUNK, D_INNER), lambda b, c: (row(b, c), 0)),
        scratch_shapes=[pltpu.VMEM((CHUNK + hist, D_INNER), F32),
                        pltpu.VMEM((CHUNK + hist, BC_W), F32),
                        pltpu.VMEM((CHUNK + hist, BC_W), F32),
                        pltpu.VMEM((D_STATE, D_INNER), F32),
                        pltpu.VMEM((CHUNK, D_INNER), F32)],
        compiler_params=_cparams("parallel", "arbitrary"),
        name="ssd_mixer",
    )(proj, proj, proj, proj, dt_raw, cwx, cwb, cwc, cbx, cbb, cbc, dtb, alog, dsk, ssd_norm[None, :], expand)


def _rwkv_pre_kernel(x_ref, xp_ref, g_ref, mix_ref, w1_ref, a1_ref, g1_ref, w2_ref, a2_ref, g2_ref, w0_ref, a0_ref,
                     o_ref, ld_ref, a_ref, gate_ref, hp_ref, *, tm, seq):
    i = pl.program_id(0)
    hist = SUBLANES
    g = g_ref[...]
    h = _rms(x_ref[...], g)
    keep = jnp.where((i * tm) % seq == 0, 0.0, 1.0)
    hp_ref[0:hist, :] = _rms(xp_ref[...], g) * keep
    hp_ref[hist:hist + tm, :] = h
    xx = hp_ref[hist - 1:hist - 1 + tm, :] - h

    def mixed(m):
        return (h + xx * mix_ref[m:m + 1, :]).astype(BF16)

    for m in range(3):
        o_ref[m] = mixed(m)
    tw = jnp.tanh(_dot(mixed(3), w1_ref[...])).astype(BF16)
    ta = _dot(mixed(4), a1_ref[...]).astype(BF16)
    tg = _sigmoid(_dot(mixed(5), g1_ref[...])).astype(BF16)
    zw = w0_ref[...] + _dot(tw, w2_ref[...])
    ld_ref[...] = _sigmoid(zw) * (-DECAY_SCALE)
    a_ref[...] = _sigmoid(a0_ref[...] + _dot(ta, a2_ref[...])).astype(a_ref.dtype)
    gate_ref[...] = _dot(tg, g2_ref[...]).astype(gate_ref.dtype)


def rwkv_pre(x, g, mix, w1, a1, g1, w2, a2, g2, w0, a0, *, tm, seq):
    t, d = x.shape
    hist = SUBLANES

    def full(a):
        return pl.BlockSpec(a.shape, lambda i: (0, 0))

    def rows():
        return pl.BlockSpec((tm, d), lambda i: (i, 0))

    return pl.pallas_call(
        functools.partial(_rwkv_pre_kernel, tm=tm, seq=seq),
        out_shape=(jax.ShapeDtypeStruct((3, t, d), BF16), jax.ShapeDtypeStruct((t, d), F32),
                   jax.ShapeDtypeStruct((t, d), BF16), jax.ShapeDtypeStruct((t, d), BF16)),
        grid=(t // tm,),
        in_specs=[rows(),
                  pl.BlockSpec((hist, d), lambda i: (jnp.maximum(i * (tm // hist) - 1, 0), 0)),
                  full(g), full(mix), full(w1), full(a1), full(g1), full(w2), full(a2), full(g2), full(w0), full(a0)],
        out_specs=(pl.BlockSpec((3, tm, d), lambda i: (0, i, 0)), rows(), rows(), rows()),
        scratch_shapes=[pltpu.VMEM((tm + hist, d), F32)],
        compiler_params=_cparams("parallel"),
        name="rwkv_pre",
    )(x, x, g, mix, w1, a1, g1, w2, a2, g2, w0, a0)


def _bmm_kernel(x_ref, w_ref, o_ref):
    o_ref[...] = _dot(x_ref[...], w_ref[...]).astype(o_ref.dtype)


def batched_matmul(xs, ws, *, tm, tn):
    nm, d, n = ws.shape
    t = xs.shape[1]
    return pl.pallas_call(
        _bmm_kernel,
        out_shape=jax.ShapeDtypeStruct((nm, t, n), BF16),
        grid=(nm, t // tm, n // tn),
        in_specs=[pl.BlockSpec((None, tm, d), lambda m, i, j: (m, i, 0)),
                  pl.BlockSpec((None, d, tn), lambda m, i, j: (m, 0, j))],
        out_specs=pl.BlockSpec((None, tm, tn), lambda m, i, j: (m, i, j)),
        compiler_params=_cparams("parallel", "parallel", "arbitrary"),
        name="batched_matmul",
    )(xs, ws)


def _rwkv_scan_kernel(r_ref, k_ref, v_ref, ld_ref, a_ref, g_ref, kk_ref, ka_ref, rk_ref, lnw_ref, lnb_ref,
                      o_ref, s_ref, *, heads, chunks):
    c = RWKV_CHUNK
    n = HEAD_DIM

    @pl.when(pl.program_id(2) == 0)
    def _():
        s_ref[...] = jnp.zeros_like(s_ref)

    ri = lax.broadcasted_iota(jnp.int32, (c, c), 0)
    ci = lax.broadcasted_iota(jnp.int32, (c, c), 1)
    tril = jnp.where(ri >= ci, 1.0, 0.0).astype(BF16)
    ri2 =lax.broadcasted_iota(jnp.int32, (2 * c, 4 * c), 0)
    ci2 = lax.broadcasted_iota(jnp.int32, (2 * c, 4 * c), 1) % c
    gmask = ci2 <= jnp.where(ri2 < c, ri2 - 1, ri2 - c)
    zeros_sq = jnp.zeros((LANES, LANES), BF16)
    n_steps = c.bit_length() - 1
    pairs = heads // 2

    block_rows, head_sums = _block_rows, _head_sums

    def all_head_sums(arr):
        return jnp.concatenate([head_sums(arr[:, j * LANES:(j + 1) * LANES]) for j in range(pairs)], axis=1)

    block_diag = (lax.broadcasted_iota(jnp.int32, (LANES, LANES), 0) < n) == _low_lanes((LANES, LANES))

    items = [(ch, j) for ch in range(chunks) for j in range(pairs)]
    at_p, rt_p, v_p, bd_v, bk_t, w_t, bonus, lhs1, rhs1 = ({} for _ in range(9))
    for ch in range(chunks):
        rows = slice(ch * c, (ch + 1) * c)
        ld = ld_ref[rows, :]
        cl = _dot_exact_rhs(tril, ld)
        cl_last = cl[c - 1:c, :]
        e_inc = jnp.exp(cl)
        e_neg = jnp.exp(-cl)
        e_exc = jnp.exp(cl - ld)
        e_bar = jnp.exp(cl_last - cl)
        w_all = jnp.exp(cl_last)
        r_all, k_all, v_all, a_all = (t_ref[rows, :].astype(F32) for t_ref in (r_ref, k_ref, v_ref, a_ref))
        kk_all = k_all * kk_ref[...]
        kk_all = kk_all * lax.rsqrt(jnp.maximum(all_head_sums(kk_all * kk_all), 1e-24))
        kf_all = k_all * (1.0 + (a_all - 1.0) * ka_ref[...])
        bb_all = kk_all * a_all
        bonus[ch] = all_head_sums(r_all * kf_all * rk_ref[...]) * v_all
        at_all = (-kk_all * e_exc).astype(BF16)
        rt_all = (r_all * e_inc).astype(BF16)
        bt_all = (bb_all * e_neg).astype(BF16)
        kt_all = (kf_all * e_neg).astype(BF16)
        vb_all = v_all.astype(BF16)
        bbar_all = bb_all * e_bar
        kbar_all = kf_all * e_bar
        w_rows = jnp.broadcast_to(w_all, (2 * c, heads * n))
        for j in range(pairs):
            it = (ch, j)
            ps = slice(j * LANES, (j + 1) * LANES)
            at_p[it], rt_p[it], v_p[it] = at_all[:, ps], rt_all[:, ps], vb_all[:, ps]
            lhs1[it] = jnp.concatenate([at_p[it], rt_p[it]], axis=0)
            rhs1[it] = jnp.concatenate([block_rows(bt_all[:, ps]), block_rows(kt_all[:, ps])], axis=0)
            bd_v[it] = block_rows(v_p[it])
            bk_t[it] = jnp.concatenate([bbar_all[:, ps], kbar_all[:, ps]], axis=0).T.astype(BF16)
            w_t[it] = w_rows[:, ps].T

    gm = {it: jnp.where(gmask, _dot_nt(lhs1[it], rhs1[it]), 0.0) for it in items}
    a_r = {it: gm[it][c:2 * c, :].astype(BF16) for it in items}
    p_b = {it: gm[it][0:c, 0:LANES].astype(BF16) for it in items}
    akv = {it: _dot(gm[it][0:c, LANES:2 * LANES].astype(BF16), bd_v[it]) for it in items}
    x = {it: jnp.concatenate([at_p[it].astype(F32), akv[it]], axis=1) for it in items}
    for step in range(n_steps):
        for it in items:
            p_cur = p_b[it]
            x[it] = x[it] + _dot(p_cur, block_rows(x[it].astype(BF16)))
            if step + 1 < n_steps:
                p_b[it] = _dot(p_cur, block_rows(p_cur)).astype(BF16)
    top = {}
    for it in items:
        rhs = jnp.concatenate([block_rows(x[it].astype(BF16)),
                               jnp.concatenate([zeros_sq, bd_v[it]], axis=1)], axis=0)
        top[it] = _dot(a_r[it], rhs)
    lhs5 = {it: jnp.concatenate([x[it][:, 0:LANES], rt_p[it].astype(F32) + top[it][:, 0:LANES]],
                                axis=0).astype(BF16) for it in items}

    for ch in range(chunks):
        rows = slice(ch * c, (ch + 1) * c)
        z0 = [s_ref[j] for j in range(pairs)]
        inter = [_dot(lhs5[(ch, j)], z0[j].astype(BF16)) for j in range(pairs)]
        for j in range(pairs):
            it = (ch, j)
            ps = slice(j * LANES, (j + 1) * LANES)
            u = inter[j][0:c] + x[it][:, LANES:2 * LANES]
            y = inter[j][c:2 * c] + top[it][:, LANES:2 * LANES]
            uv = jnp.concatenate([u.astype(BF16), v_p[it]], axis=0)
            s_ref[j] = z0[j] * w_t[it] + jnp.where(block_diag, _dot(bk_t[it], uv), 0.0)
            yc = y - head_sums(y) * (1.0 / n)
            yn = yc * lax.rsqrt(head_sums(yc * yc) * (1.0 / n) + GN_EPS)
            out = (yn * lnw_ref[:, ps] + lnb_ref[:, ps] + bonus[ch][:, ps]) * g_ref[rows, ps].astype(F32)
            o_ref[rows, ps] = out.astype(o_ref.dtype)


def rwkv_scan(rkv, ld, a, g, k_k, k_a, r_k, ln_w, ln_b, *, batch, seq, heads, chunks):
    t, d = ld.shape
    rows = chunks * RWKV_CHUNK
    nblk = seq // rows
    width = heads * HEAD_DIM

    def data(b, hg, c):
        return (b * nblk + c, hg)

    def stacked(m):
        return pl.BlockSpec((None, rows, width), lambda b, hg, c: (m, b * nblk + c, hg))

    def param():
        return pl.BlockSpec((1, width), lambda b, hg, c: (0, hg))

    return pl.pallas_call(
        functools.partial(_rwkv_scan_kernel, heads=heads, chunks=chunks),
        out_shape=jax.ShapeDtypeStruct((t, d), BF16),
        grid=(batch, d // width, nblk),
        in_specs=[stacked(0), stacked(1), stacked(2),
                  pl.BlockSpec((rows, width), data), pl.BlockSpec((rows, width), data),
                  pl.BlockSpec((rows, width), data),
                  param(), param(), param(), param(), param()],
        out_specs=pl.BlockSpec((rows, width), data),
        scratch_shapes=[pltpu.VMEM((heads // 2, LANES, LANES), F32)],
        compiler_params=_cparams("parallel", "parallel", "arbitrary"),
        name="rwkv_scan",
    )(rkv, rkv, rkv, ld, a, g, k_k[None, :], k_a[None, :], r_k.reshape(1, d), ln_w[None, :], ln_b[None, :])


def _attn_ssd_layer(x, batch, seq, norm_g, w_in, conv_w, conv_b, dt_bias, a_log, d_skip, ssd_norm,
                    q_norm, k_norm, sinks, w_out, cos_t, sin_t):
    q_end, k_end, v_end, z_end, xbc_end = 1024, 1280, 1536, 2560, 4096
    x_end = z_end + D_INNER
    w_in_p = jnp.concatenate(
        [w_in[:, :q_end], w_in[:, v_end:z_end], w_in[:, z_end:x_end], w_in[:, q_end:k_end], w_in[:, k_end:v_end],
         w_in[:, x_end:xbc_end]], axis=1).astype(BF16)
    w_dt = jnp.pad(w_in[:, xbc_end:], ((0, 0), (0, LANES - SSM_HEADS))).astype(BF16)
    proj, dt_raw = norm_matmul(x, norm_g[None, :], w_in_p, w_dt, tm=_rows(x, TM_PROJ), tn=TN_PROJ)
    y_attn = swa_attention(proj, cos_t, sin_t, jnp.tile(q_norm, 2)[None, :], jnp.tile(k_norm, 2)[None, :], sinks,
                           batch=batch, seq=seq)
    y_ssd = ssd_mixer(proj, dt_raw, conv_w, conv_b, dt_bias, a_log, d_skip, ssd_norm, batch=batch, seq=seq)
    w_out_b = w_out.astype(BF16)
    return matmul_residual(x, [y_attn, y_ssd], [w_out_b[:Q_W], w_out_b[Q_W:]], tm=_rows(x, TM_OUT), tn=D_MODEL)


def _rwkv_layer(x, batch, seq, norm_g, mix, w0, w1, w2, a0, a1, a2, g1, g2, k_k, k_a, r_k,
                w_r, w_k, w_v, w_o, ln_w, ln_b):
    mix_p = jnp.stack([mix[0], mix[2], mix[3], mix[1], mix[4], mix[5]])
    lora_pad = LANES - w1.shape[1]
    w1_p = jnp.pad(w1, ((0, 0), (0, lora_pad))).astype(BF16)
    a1_p = jnp.pad(a1, ((0, 0), (0, lora_pad))).astype(BF16)
    w2_p = jnp.pad(w2, ((0, lora_pad), (0, 0))).astype(BF16)
    a2_p = jnp.pad(a2, ((0, lora_pad), (0, 0))).astype(BF16)
    xs, ld, a, g = rwkv_pre(x, norm_g[None, :], mix_p, w1_p, a1_p, g1.astype(BF16), w2_p, a2_p, g2.astype(BF16),
                            w0[None, :], a0[None, :], tm=_rows(x, TM_PRE), seq=seq)
    rkv = batched_matmul(xs, jnp.stack([w_r, w_k, w_v]).astype(BF16), tm=_rows(x, TM_BMM), tn=D_MODEL)
    yg = rwkv_scan(rkv, ld, a, g, k_k, k_a, r_k, ln_w, ln_b, batch=batch, seq=seq, heads=16, chunks=4)
    return matmul_residual(x, [yg], [w_o.astype(BF16)], tm=_rows(x, TM_OUT), tn=D_MODEL)


def kernel(x, norm_mix, norm_ffn, w_up, w_down, ev_w_in, ev_conv_w, ev_conv_b, ev_dt_bias, ev_a_log, ev_d_skip, ev_ssd_norm, ev_q_norm, ev_k_norm, ev_sinks, ev_w_out, od_mix, od_w0, od_w1, od_w2, od_a0, od_a1, od_a2, od_g1, od_g2, od_k_k, od_k_a, od_r_k, od_w_r, od_w_k, od_w_v, od_w_o, od_ln_w, od_ln_b):
    batch, seq, d = x.shape
    depth = norm_mix.shape[0]
    inv_freq = ROPE_THETA ** (-jnp.arange(0, HEAD_DIM, 2, dtype=F32) / HEAD_DIM)
    ang = jnp.arange(seq, dtype=F32)[:, None] * inv_freq[None, :]
    cos, sin = jnp.cos(ang), jnp.sin(ang)
    cos_t = jnp.concatenate([cos, cos, cos, cos], axis=1)
    sin_t = jnp.concatenate([-sin, sin, -sin, sin], axis=1)
    xf = x.reshape(batch * seq, d)
    w_up_b, w_down_b = w_up.astype(BF16), w_down.astype(BF16)
    for i in range(depth):
        j = i // 2
        if i % 2 == 0:
            xf = _attn_ssd_layer(xf, batch, seq, norm_mix[i], ev_w_in[j], ev_conv_w[j], ev_conv_b[j],
                                 ev_dt_bias[j], ev_a_log[j], ev_d_skip[j], ev_ssd_norm[j], ev_q_norm[j],
                                 ev_k_norm[j], ev_sinks[j], ev_w_out[j], cos_t, sin_t)
        else:
            xf = _rwkv_layer(xf, batch, seq, norm_mix[i], od_mix[j], od_w0[j], od_w1[j], od_w2[j], od_a0[j],
                             od_a1[j], od_a2[j], od_g1[j], od_g2[j], od_k_k[j], od_k_a[j], od_r_k[j],
                             od_w_r[j], od_w_k[j], od_w_v[j], od_w_o[j], od_ln_w[j], od_ln_b[j])
        xf = mlp_block(xf, norm_ffn[i][None, :], w_up_b, w_down_b, i, tm=_rows(xf, TM_MLP), tf=TF_MLP)
    return xf.reshape(batch, seq, d)
```

```python
import functools

import jax
import jax.numpy as jnp
from jax import lax
from jax.experimental import pallas as pl
from jax.experimental.pallas import tpu as pltpu

F32 = jnp.float32
BF16 = jnp.bfloat16

D_MODEL = 2048
HEAD_DIM = 64
N_Q_HEADS = 16
N_KV_HEADS = 4
Q_REP = N_Q_HEADS // N_KV_HEADS
WINDOW = 128
ROPE_THETA = 10000.0
SSM_HEADS = 16
SSM_HEAD_DIM = 64
D_INNER = 1024
D_STATE = 128
N_GROUPS = 2
CONV_WIDTH = 4
CHUNK = 128
RWKV_HEADS = 32
D_FF = 4 * D_MODEL
NORM_EPS = 1e-6
GN_EPS = 64e-5
Q_W = 1024
KV_W = 256
BC_W = 256

LANES = 128
SUBLANES = 8
VMEM_LIMIT_BYTES = 56 * 1024 * 1024

DECAY_SCALE = 0.6065306597126334
RWKV_CHUNK = 64


TM_PROJ = 1024
TN_PROJ = 1024
TM_OUT = 512
TM_BMM = 1024
TM_PRE = 256
TM_MLP = 1024
TF_MLP = 512


def _rows(x, want):
    return min(want, x.shape[0])


def _cparams(*sem):
    return pltpu.CompilerParams(dimension_semantics=sem, vmem_limit_bytes=VMEM_LIMIT_BYTES)


def _rms(x, g):
    inv = lax.rsqrt(jnp.mean(x * x, axis=-1, keepdims=True) + NORM_EPS)
    return x * inv * g


def _silu(x):
    return x * (1.0 / (1.0 + jnp.exp(-x)))


def _sigmoid(x):
    return 1.0 / (1.0 + jnp.exp(-x))


def _softplus(x):
    return jnp.maximum(x, 0.0) + jnp.log1p(jnp.exp(-jnp.abs(x)))


def _dot(a, b):
    return jnp.dot(a, b, preferred_element_type=F32)


def _dot_nt(a, b):
    return lax.dot_general(a, b, (((1,), (1,)), ((), ())), preferred_element_type=F32)


def _dot_tn(a, b):
    return lax.dot_general(a, b, (((0,), (0,)), ((), ())), preferred_element_type=F32)


def _split3(x):
    hi = x.astype(BF16)
    r1 = x - hi.astype(F32)
    mid = r1.astype(BF16)
    lo = (r1 - mid.astype(F32)).astype(BF16)
    return hi, mid, lo


def _dot_exact_rhs(sel_b, x):
    hi, mid, lo = _split3(x)
    return _dot(sel_b, hi) + _dot(sel_b, mid) + _dot(sel_b, lo)


def _dot_exact_lhs(x, sel_b):
    hi, mid, lo = _split3(x)
    return _dot(hi, sel_b) + _dot(mid, sel_b) + _dot(lo, sel_b)


def _norm_matmul_kernel(x_ref, g_ref, w_ref, ws_ref, o_ref, side_ref, h_ref):
    @pl.when(pl.program_id(1) == 0)
    def _():
        h_ref[...] = _rms(x_ref[...], g_ref[...]).astype(BF16)
        side_ref[...] = _dot(h_ref[...], ws_ref[...])

    o_ref[...] = _dot(h_ref[...], w_ref[...]).astype(o_ref.dtype)


def norm_matmul(x, g, w, w_side, *, tm, tn):
    t, d = x.shape
    n = w.shape[1]
    ns = w_side.shape[1]
    return pl.pallas_call(
        _norm_matmul_kernel,
        out_shape=(jax.ShapeDtypeStruct((t, n), BF16), jax.ShapeDtypeStruct((t, ns), F32)),
        grid=(t // tm, n // tn),
        in_specs=[pl.BlockSpec((tm, d), lambda i, j: (i, 0)),
                  pl.BlockSpec((1, d), lambda i, j: (0, 0)),
                  pl.BlockSpec((d, tn), lambda i, j: (0, j)),
                  pl.BlockSpec((d, ns), lambda i, j: (0, 0))],
        out_specs=(pl.BlockSpec((tm, tn), lambda i, j: (i, j)), pl.BlockSpec((tm, ns), lambda i, j: (i, 0))),
        scratch_shapes=[pltpu.VMEM((tm, d), BF16)],
        compiler_params=_cparams("parallel", "arbitrary"),
        name="norm_matmul",
    )(x, g, w, w_side)


def _matmul_residual_kernel(*refs, n_in):
    res_ref = refs[0]
    a_refs = refs[1:1 + n_in]
    w_refs = refs[1 + n_in:1 + 2 * n_in]
    o_ref = refs[1 + 2 * n_in]
    acc = res_ref[...]
    for a_ref, w_ref in zip(a_refs, w_refs):
        acc = acc + _dot(a_ref[...], w_ref[...])
    o_ref[...] = acc


def matmul_residual(res, a_list, w_list, *, tm, tn):
    t, n = res.shape
    n_in = len(a_list)
    in_specs = [pl.BlockSpec((tm, tn), lambda i, j: (i, j))]
    in_specs += [pl.BlockSpec((tm, a.shape[1]), lambda i, j: (i, 0)) for a in a_list]
    in_specs += [pl.BlockSpec((w.shape[0], tn), lambda i, j: (0, j)) for w in w_list]
    return pl.pallas_call(
        functools.partial(_matmul_residual_kernel, n_in=n_in),
        out_shape=jax.ShapeDtypeStruct((t, n), F32),
        grid=(t // tm, n // tn),
        in_specs=in_specs,
        out_specs=pl.BlockSpec((tm, tn), lambda i, j: (i, j)),
        compiler_params=_cparams("parallel", "arbitrary"),
        name="matmul_residual",
    )(res, *a_list, *w_list)


def _mlp_kernel(x_ref, g_ref, wu_ref, wd_ref, o_ref, h_ref):
    @pl.when(pl.program_id(1) == 0)
    def _():
        x = x_ref[...]
        h_ref[...] = _rms(x, g_ref[...]).astype(BF16)
        o_ref[...] = x

    u = _dot(h_ref[...], wu_ref[...])
    a = jnp.square(jnp.maximum(u, 0.0)).astype(BF16)
    o_ref[...] += _dot(a, wd_ref[...])


def mlp_block(x, g, w_up, w_down, layer, *, tm, tf):
    t, d = x.shape
    f = w_up.shape[2]
    return pl.pallas_call(
        _mlp_kernel,
        out_shape=jax.ShapeDtypeStruct((t, d), F32),
        grid=(t // tm, f // tf),
        in_specs=[pl.BlockSpec((tm, d), lambda i, j: (i, 0)),
                  pl.BlockSpec((1, d), lambda i, j: (0, 0)),
                  pl.BlockSpec((None, d, tf), lambda i, j: (layer, 0, j)),
                  pl.BlockSpec((None, tf, d), lambda i, j: (layer, j, 0))],
        out_specs=pl.BlockSpec((tm, d), lambda i, j: (i, 0)),
        scratch_shapes=[pltpu.VMEM((tm, d), BF16)],
        compiler_params=_cparams("parallel", "arbitrary"),
        name="mlp_block",
    )(x, g, w_up, w_down)


def _low_lanes(shape):
    return lax.broadcasted_iota(jnp.int32, shape, 1) % LANES < HEAD_DIM


def _block_rows(arr):
    low = _low_lanes(arr.shape)
    zero = jnp.zeros_like(arr)
    return jnp.concatenate([jnp.where(low, arr, zero), jnp.where(low, zero, arr)], axis=0)


def _head_sums(blk):
    low = _low_lanes(blk.shape)
    s_lo = jnp.sum(jnp.where(low, blk, 0.0), axis=-1, keepdims=True)
    s_hi = jnp.sum(jnp.where(low, 0.0, blk), axis=-1, keepdims=True)
    return jnp.where(low, s_lo, s_hi)


def _both_halves(pair):
    swapped = pltpu.roll(pair, HEAD_DIM, axis=1)
    low = _low_lanes(pair.shape)
    return jnp.where(low, pair, swapped), jnp.where(low, swapped, pair)


def _norm_rope(xp, gain, cos, sin):
    half = HEAD_DIM // 2
    ms = _head_sums(xp * xp) * (1.0 / HEAD_DIM)
    xn = xp * lax.rsqrt(ms + NORM_EPS) * gain
    from_below = pltpu.roll(xn, half, axis=1)
    from_above = pltpu.roll(xn, LANES - half, axis=1)
    first_half = lax.broadcasted_iota(jnp.int32, xp.shape, 1) % HEAD_DIM < half
    return xn * cos + jnp.where(first_half, from_above, from_below) * sin


def _attn_kernel(sinks_ref, q_ref, kc_ref, vc_ref, cc_ref, sc_ref, qn_ref, kn_ref, o_ref, kprev_ref, vprev_ref):
    n = pl.program_id(1)
    w = WINDOW
    cos_c, sin_c = cc_ref[...], sc_ref[...]
    qn, kn = qn_ref[...], kn_ref[...]

    @pl.when(n == 0)
    def _():
        kprev_ref[...] = jnp.zeros_like(kprev_ref)
        vprev_ref[...] = jnp.zeros_like(vprev_ref)

    rows = lax.broadcasted_iota(jnp.int32, (Q_REP * w, 2 * w), 0)
    qi = rows % w
    kj = lax.broadcasted_iota(jnp.int32, (Q_REP * w, 2 * w), 1)
    lo = jnp.where(n == 0, w - 1, 0)
    mask = (kj > qi) & (kj <= qi + w) & (kj > lo)

    k_dup, v_dup = [], []
    for j in range(N_KV_HEADS // 2):
        ps = slice(j * LANES, (j + 1) * LANES)
        k_cur = [t.astype(BF16) for t in _both_halves(_norm_rope(kc_ref[:, ps].astype(F32), kn, cos_c, sin_c))]
        v_cur = [t.astype(BF16) for t in _both_halves(vc_ref[:, ps].astype(F32))]
        for hh in range(2):
            g = 2 * j + hh
            k_dup.append(jnp.concatenate([kprev_ref[g], k_cur[hh]], axis=0))
            v_dup.append(jnp.concatenate([vprev_ref[g], v_cur[hh]], axis=0))
            kprev_ref[g] = k_cur[hh]
            vprev_ref[g] = v_cur[hh]
    q_rows = []
    for i in range(N_Q_HEADS // 2):
        qp = _norm_rope(q_ref[:, i * LANES:(i + 1) * LANES].astype(F32), qn, cos_c, sin_c) * (HEAD_DIM ** -0.5)
        q_rows.append(_block_rows(qp.astype(BF16)))
    s = [_dot_nt(jnp.concatenate(q_rows[2 * g:2 * g + 2], axis=0), k_dup[g]) for g in range(N_KV_HEADS)]
    p, denom = [], []
    for g in range(N_KV_HEADS):
        sink = jnp.concatenate([jnp.full((w, 1), sinks_ref[g * Q_REP + r], F32) for r in range(Q_REP)], axis=0)
        sg = jnp.where(mask, s[g], -jnp.inf)
        m = jnp.maximum(jnp.max(sg, axis=-1, keepdims=True), sink)
        e = jnp.exp(sg - m)
        denom.append(jnp.sum(e, axis=-1, keepdims=True) + jnp.exp(sink - m))
        p.append(e.astype(BF16))
    o = [_dot(p[g], v_dup[g]) / denom[g] for g in range(N_KV_HEADS)]
    low = _low_lanes((w, LANES))
    for i in range(N_Q_HEADS // 2):
        og = o[i // 2]
        r0 = (i % 2) * 2 * w
        o_ref[:, i * LANES:(i + 1) * LANES] = jnp.where(low, og[r0:r0 + w], og[r0 + w:r0 + 2 * w]).astype(o_ref.dtype)


def swa_attention(proj, cos_t, sin_t, q_norm, k_norm, sinks, *, batch, seq):
    nb = seq // WINDOW
    kcol = 3 * 1024 // KV_W
    vcol = kcol + 1

    def cur(b, n):
        return b * nb + n

    return pl.pallas_call(
        _attn_kernel,
        out_shape=jax.ShapeDtypeStruct((batch * seq, Q_W), BF16),
        grid=(batch, nb),
        in_specs=[pl.BlockSpec(memory_space=pltpu.SMEM),
                  pl.BlockSpec((WINDOW, Q_W), lambda b, n: (cur(b, n), 0)),
                  pl.BlockSpec((WINDOW, KV_W), lambda b, n: (cur(b, n), kcol)),
                  pl.BlockSpec((WINDOW, KV_W), lambda b, n: (cur(b, n), vcol)),
                  pl.BlockSpec((WINDOW, LANES), lambda b, n: (n, 0)),
                  pl.BlockSpec((WINDOW, LANES), lambda b, n: (n, 0)),
                  pl.BlockSpec((1, LANES), lambda b, n: (0, 0)),
                  pl.BlockSpec((1, LANES), lambda b, n: (0, 0))],
        out_specs=pl.BlockSpec((WINDOW, Q_W), lambda b, n: (cur(b, n), 0)),
        scratch_shapes=[pltpu.VMEM((N_KV_HEADS, WINDOW, LANES), BF16),
                        pltpu.VMEM((N_KV_HEADS, WINDOW, LANES), BF16)],
        compiler_params=_cparams("parallel", "arbitrary"),
        name="swa_attention",
    )(sinks, proj, proj, proj, cos_t, sin_t, q_norm, k_norm)


def _ssd_kernel(z_ref, x_ref, b_ref, c_ref, dt_ref, cwx_ref, cwb_ref, cwc_ref, cbx_ref, cbb_ref, cbc_ref,
                dtb_ref, alog_ref, dsk_ref, nrm_ref, exp_ref, o_ref,
                xpx_ref, xpb_ref, xpc_ref, state_ref, y_ref):
    q = CHUNK
    hist = SUBLANES

    @pl.when(pl.program_id(1) == 0)
    def _():
        xpx_ref[0:hist, :] = jnp.zeros((hist, D_INNER), F32)
        xpb_ref[0:hist, :] = jnp.zeros((hist, BC_W), F32)
        xpc_ref[0:hist, :] = jnp.zeros((hist, BC_W), F32)
        state_ref[...] = jnp.zeros_like(state_ref)

    def conv(raw_ref, pad_ref, cw_ref, cb_ref):
        pad_ref[hist:hist + q, :] = raw_ref[...].astype(F32)
        acc = cb_ref[...]
        for j in range(CONV_WIDTH):
            start = hist - (CONV_WIDTH - 1) + j
            acc = acc + pad_ref[start:start + q, :] * cw_ref[j:j + 1, :]
        pad_ref[0:hist, :] = pad_ref[q:q + hist, :]
        return _silu(acc)

    xs = conv(x_ref, xpx_ref, cwx_ref, cbx_ref)
    bm = conv(b_ref, xpb_ref, cwb_ref, cbb_ref)
    cm = conv(c_ref, xpc_ref, cwc_ref, cbc_ref)

    dt = _softplus(dt_ref[...] + dtb_ref[...])
    dta = dt * (-jnp.exp(alog_ref[...]))
    ri = lax.broadcasted_iota(jnp.int32, (q, q), 0)
    ci = lax.broadcasted_iota(jnp.int32, (q, q), 1)
    causal = ri >= ci
    tril = jnp.where(causal, 1.0, 0.0).astype(BF16)
    cs = _dot_exact_rhs(tril, dta)
    cs_t = cs.T
    cs_last = cs[q - 1:q, :]
    expand = exp_ref[...]
    stacked = jnp.concatenate([dt, jnp.exp(cs), jnp.exp(cs_last - cs)], axis=0).astype(BF16)
    wide = _dot(stacked, expand)
    dt_w, ecs_w, dout_w = wide[0:q], wide[q:2 * q], wide[2 * q:3 * q]
    cdec_w = _dot_exact_lhs(jnp.broadcast_to(jnp.exp(cs_last), (SUBLANES, LANES)), expand)[0:1]

    xdt = xs * dt_w
    xdt_b = xdt.astype(BF16)
    xdec_b = (xdt * dout_w).astype(BF16)
    hg = SSM_HEADS // N_GROUPS
    gw = hg * SSM_HEAD_DIM
    for g in range(N_GROUPS):
        cg = cm[:, g * D_STATE:(g + 1) * D_STATE].astype(BF16)
        bg = bm[:, g * D_STATE:(g + 1) * D_STATE].astype(BF16)
        cb = _dot_nt(cg, bg)
        gs = slice(g * gw, (g + 1) * gw)
        prev = state_ref[:, gs]
        y_ref[:, gs] = _dot(cg, prev.astype(BF16)) * ecs_w[:, gs]
        for hh in range(hg):
            h = g * hg + hh
            seg = cs[:, h:h + 1] - cs_t[h:h + 1, :]
            lm = jnp.exp(jnp.where(causal, seg, -jnp.inf))
            hs = slice(h * SSM_HEAD_DIM, (h + 1) * SSM_HEAD_DIM)
            y_ref[:, hs] += _dot((cb * lm).astype(BF16), xdt_b[:, hs])
        state_ref[:, gs] = prev * cdec_w[:, gs] + _dot_tn(bg, xdec_b[:, gs])

    y = y_ref[...] + dsk_ref[...] * xs
    y = y * _silu(z_ref[...].astype(F32))
    for g in range(N_GROUPS):
        gs = slice(g * gw, (g + 1) * gw)
        yg = y[:, gs]
        inv = lax.rsqrt(jnp.mean(yg * yg, axis=-1, keepdims=True) + NORM_EPS)
        o_ref[:, gs] = (yg * inv * nrm_ref[:, gs]).astype(o_ref.dtype)


def ssd_mixer(proj, dt_raw, conv_w, conv_b, dt_bias, a_log, d_skip, ssd_norm, *, batch, seq):
    nc = seq // CHUNK
    pad = LANES - SSM_HEADS
    cwx, cwb, cwc = conv_w[:, :D_INNER], conv_w[:, D_INNER:D_INNER + BC_W], conv_w[:, D_INNER + BC_W:]
    cbx, cbb, cbc = (conv_b[None, :D_INNER], conv_b[None, D_INNER:D_INNER + BC_W], conv_b[None, D_INNER + BC_W:])
    dtb = jnp.pad(dt_bias, (0, pad))[None, :]
    alog = jnp.pad(a_log, (0, pad))[None, :]
    dsk = jnp.repeat(d_skip, SSM_HEAD_DIM)[None, :]
    head_of_channel = jnp.arange(D_INNER) // SSM_HEAD_DIM
    expand = (jnp.arange(LANES)[:, None] == head_of_channel[None, :]).astype(BF16)

    def row(b, c):
        return b * nc + c

    def full(shape):
        return pl.BlockSpec(shape, lambda b, c: (0, 0))

    hist = SUBLANES
    return pl.pallas_call(
        _ssd_kernel,
        out_shape=jax.ShapeDtypeStruct((batch * seq, D_INNER), BF16),
        grid=(batch, nc),
        in_specs=[pl.BlockSpec((CHUNK, D_INNER), lambda b, c: (row(b, c), 1)),
                  pl.BlockSpec((CHUNK, D_INNER), lambda b, c: (row(b, c), 2)),
                  pl.BlockSpec((CHUNK, BC_W), lambda b, c: (row(b, c), 14)),
                  pl.BlockSpec((CHUNK, BC_W), lambda b, c: (row(b, c), 15)),
                  pl.BlockSpec((CHUNK, LANES), lambda b, c: (row(b, c), 0)),
                  full((CONV_WIDTH, D_INNER)), full((CONV_WIDTH, BC_W)), full((CONV_WIDTH, BC_W)),
                  full((1, D_INNER)), full((1, BC_W)), full((1, BC_W)),
                  full((1, LANES)), full((1, LANES)), full((1, D_INNER)), full((1, D_INNER)),
                  full((LANES, D_INNER))],
        out_specs=pl.BlockSpec((CHUNK, D_INNER), lambda b, c: (row(b, c), 0)),
        scratch_shapes=[pltpu.VMEM((CHUNK + hist, D_INNER), F32),
                        pltpu.VMEM((CHUNK + hist, BC_W), F32),
                        pltpu.VMEM((CHUNK + hist, BC_W), F32),
                        pltpu.VMEM((D_STATE, D_INNER), F32),
                        pltpu.VMEM((CHUNK, D_INNER), F32)],
        compiler_params=_cparams("parallel", "arbitrary"),
        name="ssd_mixer",
    )(proj, proj, proj, proj, dt_raw, cwx, cwb, cwc, cbx, cbb, cbc, dtb, alog, dsk, ssd_norm[None, :], expand)


def _rwkv_pre_kernel(x_ref, xp_ref, g_ref, mix_ref, w1_ref, a1_ref, g1_ref, w2_ref, a2_ref, g2_ref, w0_ref, a0_ref,
                     o_ref, ld_ref, a_ref, gate_ref, hp_ref, *, tm, seq):
    i = pl.program_id(0)
    hist = SUBLANES
    g = g_ref[...]
    h = _rms(x_ref[...], g)
    keep = jnp.where((i * tm) % seq == 0, 0.0, 1.0)
    hp_ref[0:hist, :] = _rms(xp_ref[...], g) * keep
    hp_ref[hist:hist + tm, :] = h
    xx = hp_ref[hist - 1:hist - 1 + tm, :] - h

    def mixed(m):
        return (h + xx * mix_ref[m:m + 1, :]).astype(BF16)

    for m in range(3):
        o_ref[m] = mixed(m)
    tw = jnp.tanh(_dot(mixed(3), w1_ref[...])).astype(BF16)
    ta = _dot(mixed(4), a1_ref[...]).astype(BF16)
    tg = _sigmoid(_dot(mixed(5), g1_ref[...])).astype(BF16)
    zw = w0_ref[...] + _dot(tw, w2_ref[...])
    ld_ref[...] = _sigmoid(zw) * (-DECAY_SCALE)
    a_ref[...] = _sigmoid(a0_ref[...] + _dot(ta, a2_ref[...])).astype(a_ref.dtype)
    gate_ref[...] = _dot(tg, g2_ref[...]).astype(gate_ref.dtype)


def rwkv_pre(x, g, mix, w1, a1, g1, w2, a2, g2, w0, a0, *, tm, seq):
    t, d = x.shape
    hist = SUBLANES

    def full(a):
        return pl.BlockSpec(a.shape, lambda i: (0, 0))

    def rows():
        return pl.BlockSpec((tm, d), lambda i: (i, 0))

    return pl.pallas_call(
        functools.partial(_rwkv_pre_kernel, tm=tm, seq=seq),
        out_shape=(jax.ShapeDtypeStruct((3, t, d), BF16), jax.ShapeDtypeStruct((t, d), F32),
                   jax.ShapeDtypeStruct((t, d), BF16), jax.ShapeDtypeStruct((t, d), BF16)),
        grid=(t // tm,),
        in_specs=[rows(),
                  pl.BlockSpec((hist, d), lambda i: (jnp.maximum(i * (tm // hist) - 1, 0), 0)),
                  full(g), full(mix), full(w1), full(a1), full(g1), full(w2), full(a2), full(g2), full(w0), full(a0)],
        out_specs=(pl.BlockSpec((3, tm, d), lambda i: (0, i, 0)), rows(), rows(), rows()),
        scratch_shapes=[pltpu.VMEM((tm + hist, d), F32)],
        compiler_params=_cparams("parallel"),
        name="rwkv_pre",
    )(x, x, g, mix, w1, a1, g1, w2, a2, g2, w0, a0)


def _bmm_kernel(x_ref, w_ref, o_ref):
    o_ref[...] = _dot(x_ref[...], w_ref[...]).astype(o_ref.dtype)


def batched_matmul(xs, ws, *, tm, tn):
    nm, d, n = ws.shape
    t = xs.shape[1]
    return pl.pallas_call(
        _bmm_kernel,
        out_shape=jax.ShapeDtypeStruct((nm, t, n), BF16),
        grid=(nm, t // tm, n // tn),
        in_specs=[pl.BlockSpec((None, tm, d), lambda m, i, j: (m, i, 0)),
                  pl.BlockSpec((None, d, tn), lambda m, i, j: (m, 0, j))],
        out_specs=pl.BlockSpec((None, tm, tn), lambda m, i, j: (m, i, j)),
        compiler_params=_cparams("parallel", "parallel", "arbitrary"),
        name="batched_matmul",
    )(xs, ws)


def _rwkv_scan_kernel(r_ref, k_ref, v_ref, ld_ref, a_ref, g_ref, kk_ref, ka_ref, rk_ref, lnw_ref, lnb_ref,
                      o_ref, s_ref, *, heads, chunks):
    c = RWKV_CHUNK
    n = HEAD_DIM

    @pl.when(pl.program_id(2) == 0)
    def _():
        s_ref[...] = jnp.zeros_like(s_ref)

    ri = lax.broadcasted_iota(jnp.int32, (c, c), 0)
    ci = lax.broadcasted_iota(jnp.int32, (c, c), 1)
    tril = jnp.where(ri >= ci, 1.0, 0.0).astype(BF16)
    ri2 =lax.broadcasted_iota(jnp.int32, (2 * c, 4 * c), 0)
    ci2 = lax.broadcasted_iota(jnp.int32, (2 * c, 4 * c), 1) % c
    gmask = ci2 <= jnp.where(ri2 < c, ri2 - 1, ri2 - c)
    zeros_sq = jnp.zeros((LANES, LANES), BF16)
    n_steps = c.bit_length() - 1
    pairs = heads // 2

    block_rows, head_sums = _block_rows, _head_sums

    def all_head_sums(arr):
        return jnp.concatenate([head_sums(arr[:, j * LANES:(j + 1) * LANES]) for j in range(pairs)], axis=1)

    block_diag = (lax.broadcasted_iota(jnp.int32, (LANES, LANES), 0) < n) == _low_lanes((LANES, LANES))

    items = [(ch, j) for ch in range(chunks) for j in range(pairs)]
    at_p, rt_p, v_p, bd_v, bk_t, w_t, bonus, lhs1, rhs1 = ({} for _ in range(9))
    for ch in range(chunks):
        rows = slice(ch * c, (ch + 1) * c)
        ld = ld_ref[rows, :]
        cl = _dot_exact_rhs(tril, ld)
        cl_last = cl[c - 1:c, :]
        e_inc = jnp.exp(cl)
        e_neg = jnp.exp(-cl)
        e_exc = jnp.exp(cl - ld)
        e_bar = jnp.exp(cl_last - cl)
        w_all = jnp.exp(cl_last)
        r_all, k_all, v_all, a_all = (t_ref[rows, :].astype(F32) for t_ref in (r_ref, k_ref, v_ref, a_ref))
        kk_all = k_all * kk_ref[...]
        kk_all = kk_all * lax.rsqrt(jnp.maximum(all_head_sums(kk_all * kk_all), 1e-24))
        kf_all = k_all * (1.0 + (a_all - 1.0) * ka_ref[...])
        bb_all = kk_all * a_all
        bonus[ch] = all_head_sums(r_all * kf_all * rk_ref[...]) * v_all
        at_all = (-kk_all * e_exc).astype(BF16)
        rt_all = (r_all * e_inc).astype(BF16)
        bt_all = (bb_all * e_neg).astype(BF16)
        kt_all = (kf_all * e_neg).astype(BF16)
        vb_all = v_all.astype(BF16)
        bbar_all = bb_all * e_bar
        kbar_all = kf_all * e_bar
        w_rows = jnp.broadcast_to(w_all, (2 * c, heads * n))
        for j in range(pairs):
            it = (ch, j)
            ps = slice(j * LANES, (j + 1) * LANES)
            at_p[it], rt_p[it], v_p[it] = at_all[:, ps], rt_all[:, ps], vb_all[:, ps]
            lhs1[it] = jnp.concatenate([at_p[it], rt_p[it]], axis=0)
            rhs1[it] = jnp.concatenate([block_rows(bt_all[:, ps]), block_rows(kt_all[:, ps])], axis=0)
            bd_v[it] = block_rows(v_p[it])
            bk_t[it] = jnp.concatenate([bbar_all[:, ps], kbar_all[:, ps]], axis=0).T.astype(BF16)
            w_t[it] = w_rows[:, ps].T

    gm = {it: jnp.where(gmask, _dot_nt(lhs1[it], rhs1[it]), 0.0) for it in items}
    a_r = {it: gm[it][c:2 * c, :].astype(BF16) for it in items}
    p_b = {it: gm[it][0:c, 0:LANES].astype(BF16) for it in items}
    akv = {it: _dot(gm[it][0:c, LANES:2 * LANES].astype(BF16), bd_v[it]) for it in items}
    x = {it: jnp.concatenate([at_p[it].astype(F32), akv[it]], axis=1) for it in items}
    for step in range(n_steps):
        for it in items:
            p_cur = p_b[it]
            x[it] = x[it] + _dot(p_cur, block_rows(x[it].astype(BF16)))
            if step + 1 < n_steps:
                p_b[it] = _dot(p_cur, block_rows(p_cur)).astype(BF16)
    top = {}
    for it in items:
        rhs = jnp.concatenate([block_rows(x[it].astype(BF16)),
                               jnp.concatenate([zeros_sq, bd_v[it]], axis=1)], axis=0)
        top[it] = _dot(a_r[it], rhs)
    lhs5 = {it: jnp.concatenate([x[it][:, 0:LANES], rt_p[it].astype(F32) + top[it][:, 0:LANES]],
                                axis=0).astype(BF16) for it in items}

    for ch in range(chunks):
        rows = slice(ch * c, (ch + 1) * c)
        z0 = [s_ref[j] for j in range(pairs)]
        inter = [_dot(lhs5[(ch, j)], z0[j].astype(BF16)) for j in range(pairs)]
        for j in range(pairs):
            it = (ch, j)
            ps = slice(j * LANES, (j + 1) * LANES)
            u = inter[j][0:c] + x[it][:, LANES:2 * LANES]
            y = inter[j][c:2 * c] + top[it][:, LANES:2 * LANES]
            uv = jnp.concatenate([u.astype(BF16), v_p[it]], axis=0)
            s_ref[j] = z0[j] * w_t[it] + jnp.where(block_diag, _dot(bk_t[it], uv), 0.0)
            yc = y - head_sums(y) * (1.0 / n)
            yn = yc * lax.rsqrt(head_sums(yc * yc) * (1.0 / n) + GN_EPS)
            out = (yn * lnw_ref[:, ps] + lnb_ref[:, ps] + bonus[ch][:, ps]) * g_ref[rows, ps].astype(F32)
            o_ref[rows, ps] = out.astype(o_ref.dtype)


def rwkv_scan(rkv, ld, a, g, k_k, k_a, r_k, ln_w, ln_b, *, batch, seq, heads, chunks):
    t, d = ld.shape
    rows = chunks * RWKV_CHUNK
    nblk = seq // rows
    width = heads * HEAD_DIM

    def data(b, hg, c):
        return (b * nblk + c, hg)

    def stacked(m):
        return pl.BlockSpec((None, rows, width), lambda b, hg, c: (m, b * nblk + c, hg))

    def param():
        return pl.BlockSpec((1, width), lambda b, hg, c: (0, hg))

    return pl.pallas_call(
        functools.partial(_rwkv_scan_kernel, heads=heads, chunks=chunks),
        out_shape=jax.ShapeDtypeStruct((t, d), BF16),
        grid=(batch, d // width, nblk),
        in_specs=[stacked(0), stacked(1), stacked(2),
                  pl.BlockSpec((rows, width), data), pl.BlockSpec((rows, width), data),
                  pl.BlockSpec((rows, width), data),
                  param(), param(), param(), param(), param()],
        out_specs=pl.BlockSpec((rows, width), data),
        scratch_shapes=[pltpu.VMEM((heads // 2, LANES, LANES), F32)],
        compiler_params=_cparams("parallel", "parallel", "arbitrary"),
        name="rwkv_scan",
    )(rkv, rkv, rkv, ld, a, g, k_k[None, :], k_a[None, :], r_k.reshape(1, d), ln_w[None, :], ln_b[None, :])


def _attn_ssd_layer(x, batch, seq, norm_g, w_in, conv_w, conv_b, dt_bias, a_log, d_skip, ssd_norm,
                    q_norm, k_norm, sinks, w_out, cos_t, sin_t):
    q_end, k_end, v_end, z_end, xbc_end = 1024, 1280, 1536, 2560, 4096
    x_end = z_end + D_INNER
    w_in_p = jnp.concatenate(
        [w_in[:, :q_end], w_in[:, v_end:z_end], w_in[:, z_end:x_end], w_in[:, q_end:k_end], w_in[:, k_end:v_end],
         w_in[:, x_end:xbc_end]], axis=1).astype(BF16)
    w_dt = jnp.pad(w_in[:, xbc_end:], ((0, 0), (0, LANES - SSM_HEADS))).astype(BF16)
    proj, dt_raw = norm_matmul(x, norm_g[None, :], w_in_p, w_dt, tm=_rows(x, TM_PROJ), tn=TN_PROJ)
    y_attn = swa_attention(proj, cos_t, sin_t, jnp.tile(q_norm, 2)[None, :], jnp.tile(k_norm, 2)[None, :], sinks,
                           batch=batch, seq=seq)
    y_ssd = ssd_mixer(proj, dt_raw, conv_w, conv_b, dt_bias, a_log, d_skip, ssd_norm, batch=batch, seq=seq)
    w_out_b = w_out.astype(BF16)
    return matmul_residual(x, [y_attn, y_ssd], [w_out_b[:Q_W], w_out_b[Q_W:]], tm=_rows(x, TM_OUT), tn=D_MODEL)


def _rwkv_layer(x, batch, seq, norm_g, mix, w0, w1, w2, a0, a1, a2, g1, g2, k_k, k_a, r_k,
                w_r, w_k, w_v, w_o, ln_w, ln_b):
    mix_p = jnp.stack([mix[0], mix[2], mix[3], mix[1], mix[4], mix[5]])
    lora_pad = LANES - w1.shape[1]
    w1_p = jnp.pad(w1, ((0, 0), (0, lora_pad))).astype(BF16)
    a1_p = jnp.pad(a1, ((0, 0), (0, lora_pad))).astype(BF16)
    w2_p = jnp.pad(w2, ((0, lora_pad), (0, 0))).astype(BF16)
    a2_p = jnp.pad(a2, ((0, lora_pad), (0, 0))).astype(BF16)
    xs, ld, a, g = rwkv_pre(x, norm_g[None, :], mix_p, w1_p, a1_p, g1.astype(BF16), w2_p, a2_p, g2.astype(BF16),
                            w0[None, :], a0[None, :], tm=_rows(x, TM_PRE), seq=seq)
    rkv = batched_matmul(xs, jnp.stack([w_r, w_k, w_v]).astype(BF16), tm=_rows(x, TM_BMM), tn=D_MODEL)
    yg = rwkv_scan(rkv, ld, a, g, k_k, k_a, r_k, ln_w, ln_b, batch=batch, seq=seq, heads=16, chunks=4)
    return matmul_residual(x, [yg], [w_o.astype(BF16)], tm=_rows(x, TM_OUT), tn=D_MODEL)


def kernel(x, norm_mix, norm_ffn, w_up, w_down, ev_w_in, ev_conv_w, ev_conv_b, ev_dt_bias, ev_a_log, ev_d_skip, ev_ssd_norm, ev_q_norm, ev_k_norm, ev_sinks, ev_w_out, od_mix, od_w0, od_w1, od_w2, od_a0, od_a1, od_a2, od_g1, od_g2, od_k_k, od_k_a, od_r_k, od_w_r, od_w_k, od_w_v, od_w_o, od_ln_w, od_ln_b):
    batch, seq, d = x.shape
    depth = norm_mix.shape[0]
    inv_freq = ROPE_THETA ** (-jnp.arange(0, HEAD_DIM, 2, dtype=F32) / HEAD_DIM)
    ang = jnp.arange(seq, dtype=F32)[:, None] * inv_freq[None, :]
    cos, sin = jnp.cos(ang), jnp.sin(ang)
    cos_t = jnp.concatenate([cos, cos, cos, cos], axis=1)
    sin_t = jnp.concatenate([-sin, sin, -sin, sin], axis=1)
    xf = x.reshape(batch * seq, d)
    w_up_b, w_down_b = w_up.astype(BF16), w_down.astype(BF16)
    for i in range(depth):
        j = i // 2
        if i % 2 == 0:
            xf = _attn_ssd_layer(xf, batch, seq, norm_mix[i], ev_w_in[j], ev_conv_w[j], ev_conv_b[j],
                                 ev_dt_bias[j], ev_a_log[j], ev_d_skip[j], ev_ssd_norm[j], ev_q_norm[j],
                                 ev_k_norm[j], ev_sinks[j], ev_w_out[j], cos_t, sin_t)
        else:
            xf = _rwkv_layer(xf, batch, seq, norm_mix[i], od_mix[j], od_w0[j], od_w1[j], od_w2[j], od_a0[j],
                             od_a1[j], od_a2[j], od_g1[j], od_g2[j], od_k_k[j], od_k_a[j], od_r_k[j],
                             od_w_r[j], od_w_k[j], od_w_v[j], od_w_o[j], od_ln_w[j], od_ln_b[j])
        xf = mlp_block(xf, norm_ffn[i][None, :], w_up_b, w_down_b, i, tm=_rows(xf, TM_MLP), tf=TF_MLP)
    return xf.reshape(batch, seq, d)
```
